```python
import jax
import jax.numpy as jnp
from jax import lax
import numpy as np

D_MODEL = 1024
BATCH = 2
SEQ = 8192
DEPTH = 2
DEC_BATCH = 8
DEC_SEQ = 8192
PAST_LEN = 128

GRID_W = 64
HEAD_DIM = 64
NA_HEADS = 6
NA_WIN_H = 8
NA_WIN_W = 16
NA_DIM = NA_HEADS * HEAD_DIM
SC_DIM = 256
SC_WIDTH = 3
SWA_HEADS = 6
SWA_KV_HEADS = 2
SWA_GROUP = SWA_HEADS // SWA_KV_HEADS
SWA_WINDOW = 128
SWA_BLOCK = 128
SWA_DIM = SWA_HEADS * HEAD_DIM
SWA_KV_DIM = SWA_KV_HEADS * HEAD_DIM
T5_BUCKETS = 32
T5_MAX_DIST = 128
MIX_DIM = NA_DIM + SC_DIM + SWA_DIM
IN_SPLITS = (NA_DIM, NA_DIM, NA_DIM, SC_DIM, SC_DIM, SC_DIM, SWA_DIM, SWA_KV_DIM, SWA_KV_DIM)
IN_DIM = 3 * NA_DIM + 3 * SC_DIM + SWA_DIM + 2 * SWA_KV_DIM
MEM_LEN = 256
XA_HEADS = 4
XA_HEAD_DIM = 128
XA_DIM = XA_HEADS * XA_HEAD_DIM
PEER_HEADS = 8
PEER_NKEYS = 128
PEER_EXPERTS = PEER_NKEYS * PEER_NKEYS
PEER_TOPK = 16
PEER_KEY_DIM = 256
PEER_HALF = PEER_KEY_DIM // 2
PEER_CHUNK = 128
RMS_EPS = 1e-6
NEG_INF = -1e30

kernel_name = "hybrid_na2d_shortconv_swa_peer_encoder"


def rmsnorm(x, g):
    xf = x.astype(jnp.float32)
    y = xf * lax.rsqrt(jnp.mean(xf * xf, axis=-1, keepdims=True) + RMS_EPS)
    return (y * g.astype(jnp.float32)).astype(x.dtype)


def t5_bucket(rel):
    nb = T5_BUCKETS // 2
    max_exact = nb // 2
    ret = (rel > 0).astype(np.int32) * nb
    n = np.abs(rel)
    large = max_exact + (np.log(np.maximum(n, 1) / max_exact) / np.log(T5_MAX_DIST / max_exact) * (nb - max_exact)).astype(np.int32)
    large = np.minimum(large, nb - 1)
    return (ret + np.where(n < max_exact, n, large)).astype(np.int32)


def neighborhood_attention(q, k, v, rpb):
    b, s = q.shape[0], q.shape[1]
    rows = s // GRID_W
    kh = min(NA_WIN_H, rows)
    grid = lambda t: t.reshape(b, rows, GRID_W, NA_HEADS, HEAD_DIM)
    qg, kg, vg = grid(q), grid(k), grid(v)
    col = np.arange(GRID_W)
    col_start = np.clip(col - NA_WIN_W // 2, 0, GRID_W - NA_WIN_W)
    col_idx = col_start[:, None] + np.arange(NA_WIN_W)[None, :]
    dc_idx = col_idx - col[:, None] + (NA_WIN_W - 1)
    scale = HEAD_DIM ** -0.5

    def row_block(r):
        r0 = jnp.clip(r - kh // 2, 0, rows - kh)
        rows_idx = r0 + jnp.arange(kh)
        k_r = jnp.take(kg, rows_idx, axis=1)[:, :, col_idx]
        v_r = jnp.take(vg, rows_idx, axis=1)[:, :, col_idx]
        q_r = lax.dynamic_index_in_dim(qg, r, axis=1, keepdims=False)
        logits = jnp.einsum("bchd,bicjhd->bhcij", q_r, k_r).astype(jnp.float32) * scale
        dr_idx = rows_idx - r + (NA_WIN_H - 1)
        bias = rpb[dr_idx[None, :, None], dc_idx[:, None, :]]
        logits = logits + jnp.transpose(bias, (3, 0, 1, 2)).astype(jnp.float32)
        p = jax.nn.softmax(logits.reshape(b, NA_HEADS, GRID_W, kh * NA_WIN_W), axis=-1).reshape(logits.shape)
        return jnp.einsum("bhcij,bicjhd->bchd", p.astype(v.dtype), v_r)

    out = lax.map(row_block, jnp.arange(rows))
    return jnp.transpose(out, (1, 0, 2, 3, 4)).reshape(b, s, NA_DIM)


def short_conv_mixer(gate_b, gate_c, hx, w):
    u = gate_c * hx
    half = SC_WIDTH // 2
    s = u.shape[1]
    up = jnp.pad(u, ((0, 0), (half, half), (0, 0)))
    y = up[:, 0:s] * w[0]
    for tap in range(1, SC_WIDTH):
        y = y + up[:, tap:tap + s] * w[tap]
    return gate_b * y


def window_gqa(q, k, v, bias_off, sink):
    b, s = q.shape[0], q.shape[1]
    blk = SWA_BLOCK
    nb = s // blk
    qb = q.reshape(b, nb, blk, SWA_KV_HEADS, SWA_GROUP, HEAD_DIM)
    pad = ((0, 0), (blk, blk), (0, 0), (0, 0))
    kp = jnp.pad(k, pad).reshape(b, nb + 2, blk, SWA_KV_HEADS, HEAD_DIM)
    vp = jnp.pad(v, pad).reshape(b, nb + 2, blk, SWA_KV_HEADS, HEAD_DIM)
    kband = jnp.concatenate([kp[:, :-2], kp[:, 1:-1], kp[:, 2:]], axis=2)
    vband = jnp.concatenate([vp[:, :-2], vp[:, 1:-1], vp[:, 2:]], axis=2)
    logits = jnp.einsum("bnqhgd,bnjhd->bnhgqj", qb, kband).astype(jnp.float32) * (HEAD_DIM ** -0.5)
    a = np.arange(blk)[:, None]
    j = np.arange(3 * blk)[None, :]
    off = j - blk - a
    in_win = np.abs(off) <= SWA_WINDOW
    key_pos = (np.arange(nb)[:, None] - 1) * blk + np.arange(3 * blk)[None, :]
    valid = (key_pos >= 0) & (key_pos < s)
    mask = in_win[None, :, :] & valid[:, None, :]
    bias = bias_off[np.clip(off + SWA_WINDOW, 0, 2 * SWA_WINDOW)]
    bias = jnp.transpose(bias.reshape(blk, 3 * blk, SWA_KV_HEADS, SWA_GROUP), (2, 3, 0, 1)).astype(jnp.float32)
    logits = jnp.where(mask[None, :, None, None], logits + bias[None, None], NEG_INF)
    sk = sink.astype(jnp.float32).reshape(1, 1, SWA_KV_HEADS, SWA_GROUP, 1, 1)
    m = jnp.maximum(jnp.max(logits, axis=-1, keepdims=True), sk)
    p = jnp.exp(logits - m)
    p = p / (jnp.sum(p, axis=-1, keepdims=True) + jnp.exp(sk - m))
    out = jnp.einsum("bnhgqj,bnjhd->bnqhgd", p.astype(v.dtype), vband)
    return out.reshape(b, s, SWA_DIM)


def mem_cross_attention(xn, memn, wq, wk, wv, wo):
    b, s = xn.shape[0], xn.shape[1]
    m = memn.shape[1]
    q = (xn @ wq).reshape(b, s, XA_HEADS, XA_HEAD_DIM)
    k = (memn @ wk).reshape(b, m, XA_HEADS, XA_HEAD_DIM)
    v = (memn @ wv).reshape(b, m, XA_HEADS, XA_HEAD_DIM)
    logits = jnp.einsum("bshd,bmhd->bhsm", q, k).astype(jnp.float32) * (XA_HEAD_DIM ** -0.5)
    p = jax.nn.softmax(logits, axis=-1)
    o = jnp.einsum("bhsm,bmhd->bshd", p.astype(v.dtype), v).reshape(b, s, XA_DIM)
    return o @ wo


def peer_ffn(xn, wq, subkeys, u, v):
    b, s, d = xn.shape
    xt = xn.reshape(-1, PEER_CHUNK, d)

    def chunk(xc):
        q = (xc @ wq).reshape(PEER_CHUNK, PEER_HEADS, 2, PEER_HALF)
        scores = jnp.einsum("thpe,hpne->thpn", q, subkeys)
        top_s, top_i = lax.top_k(scores, PEER_TOPK)
        cand = top_s[:, :, 0, :, None] + top_s[:, :, 1, None, :]
        cand_idx = top_i[:, :, 0, :, None] * PEER_NKEYS + top_i[:, :, 1, None, :]
        cand = cand.reshape(PEER_CHUNK, PEER_HEADS, PEER_TOPK * PEER_TOPK)
        cand_idx = cand_idx.reshape(PEER_CHUNK, PEER_HEADS, PEER_TOPK * PEER_TOPK)
        best_s, best_pos = lax.top_k(cand, PEER_TOPK)
        expert = jnp.take_along_axis(cand_idx, best_pos, axis=-1)
        gate = jax.nn.softmax(best_s.astype(jnp.float32), axis=-1)
        act = jax.nn.gelu(jnp.einsum("thkd,td->thk", u[expert], xc).astype(jnp.float32), approximate=False)
        w = (gate * act).astype(xc.dtype)
        return jnp.einsum("thk,thkd->td", w, v[expert])

    return lax.map(chunk, xt).reshape(b, s, d)


def encoder_trunk(x, mem, norm_mix_g, w_in, na_rpb, conv_w, swa_sink, t5_bias, w_out,
                  norm_xa_g, norm_mem_g, w_xq, w_xk, w_xv, w_xo,
                  norm_ffn_g, peer_wq, peer_subkeys, peer_u, peer_v, final_g):
    b, s = x.shape[0], x.shape[1]
    offsets = np.arange(-SWA_WINDOW, SWA_WINDOW + 1)
    bias_off = t5_bias[t5_bucket(offsets)]
    split_at = [int(i) for i in np.cumsum(IN_SPLITS)[:-1]]
    for l in range(DEPTH):
        h = rmsnorm(x, norm_mix_g[l])
        z = h @ w_in[l]
        na_q, na_k, na_v, sc_b, sc_c, sc_h, sw_q, sw_k, sw_v = jnp.split(z, split_at, axis=-1)
        y_na = neighborhood_attention(na_q.reshape(b, s, NA_HEADS, HEAD_DIM),
                                      na_k.reshape(b, s, NA_HEADS, HEAD_DIM),
                                      na_v.reshape(b, s, NA_HEADS, HEAD_DIM), na_rpb[l])
        y_sc = short_conv_mixer(sc_b, sc_c, sc_h, conv_w[l])
        y_sw = window_gqa(sw_q.reshape(b, s, SWA_HEADS, HEAD_DIM),
                          sw_k.reshape(b, s, SWA_KV_HEADS, HEAD_DIM),
                          sw_v.reshape(b, s, SWA_KV_HEADS, HEAD_DIM), bias_off, swa_sink[l])
        x = x + jnp.concatenate([y_na, y_sc, y_sw], axis=-1) @ w_out[l]
        x = x + mem_cross_attention(rmsnorm(x, norm_xa_g[l]), rmsnorm(mem, norm_mem_g[l]),
                                    w_xq[l], w_xk[l], w_xv[l], w_xo[l])
        x = x + peer_ffn(rmsnorm(x, norm_ffn_g[l]), peer_wq[l], peer_subkeys[l], peer_u[l], peer_v[l])
    return rmsnorm(x, final_g)


def setup_inputs(seed: int = 0) -> dict:
    key = jax.random.key(seed)
    ks = jax.random.split(key, 24)
    f32 = jnp.float32
    nrm = lambda k, shape, scale: jax.random.normal(k, shape, f32) * scale
    gain = lambda k, shape: 1.0 + 0.02 * jax.random.normal(k, shape, f32)
    return {
        "x_prompt": nrm(ks[0], (BATCH, SEQ, D_MODEL), 1.0),
        "x_sample": nrm(ks[1], (DEC_BATCH, DEC_SEQ, D_MODEL), 1.0),
        "mem_prompt": nrm(ks[2], (BATCH, MEM_LEN, D_MODEL), 1.0),
        "mem_sample": nrm(ks[3], (DEC_BATCH, MEM_LEN, D_MODEL), 1.0),
        "norm_mix_g": gain(ks[4], (DEPTH, D_MODEL)),
        "w_in": nrm(ks[5], (DEPTH, D_MODEL, IN_DIM), D_MODEL ** -0.5),
        "na_rpb": nrm(ks[6], (DEPTH, 2 * NA_WIN_H - 1, 2 * NA_WIN_W - 1, NA_HEADS), 0.1),
        "conv_w": nrm(ks[7], (DEPTH, SC_WIDTH, SC_DIM), SC_WIDTH ** -0.5),
        "swa_sink": nrm(ks[8], (DEPTH, SWA_HEADS), 0.5),
        "t5_bias": nrm(ks[9], (T5_BUCKETS, SWA_HEADS), 0.1),
        "w_out": nrm(ks[10], (DEPTH, MIX_DIM, D_MODEL), MIX_DIM ** -0.5),
        "norm_xa_g": gain(ks[11], (DEPTH, D_MODEL)),
        "norm_mem_g": gain(ks[12], (DEPTH, D_MODEL)),
        "w_xq": nrm(ks[13], (DEPTH, D_MODEL, XA_DIM), D_MODEL ** -0.5),
        "w_xk": nrm(ks[14], (DEPTH, D_MODEL, XA_DIM), D_MODEL ** -0.5),
        "w_xv": nrm(ks[15], (DEPTH, D_MODEL, XA_DIM), D_MODEL ** -0.5),
        "w_xo": nrm(ks[16], (DEPTH, XA_DIM, D_MODEL), XA_DIM ** -0.5),
        "norm_ffn_g": gain(ks[17], (DEPTH, D_MODEL)),
        "peer_wq": nrm(ks[18], (DEPTH, D_MODEL, PEER_HEADS * PEER_KEY_DIM), D_MODEL ** -0.5),
        "peer_subkeys": nrm(ks[19], (DEPTH, PEER_HEADS, 2, PEER_NKEYS, PEER_HALF), PEER_HALF ** -0.5),
        "peer_u": nrm(ks[20], (DEPTH, PEER_EXPERTS, D_MODEL), D_MODEL ** -0.5),
        "peer_v": nrm(ks[21], (DEPTH, PEER_EXPERTS, D_MODEL), D_MODEL ** -0.5),
        "final_g": gain(ks[22], (D_MODEL,)),
    }


def reference(x_prompt, x_sample, mem_prompt, mem_sample, norm_mix_g, w_in, na_rpb, conv_w,
              swa_sink, t5_bias, w_out, norm_xa_g, norm_mem_g, w_xq, w_xk, w_xv, w_xo,
              norm_ffn_g, peer_wq, peer_subkeys, peer_u, peer_v, final_g):
    y_prompt = encoder_trunk(x_prompt, mem_prompt, norm_mix_g, w_in, na_rpb, conv_w, swa_sink, t5_bias, w_out,
                             norm_xa_g, norm_mem_g, w_xq, w_xk, w_xv, w_xo,
                             norm_ffn_g, peer_wq, peer_subkeys, peer_u, peer_v, final_g)
    y_sample = encoder_trunk(x_sample, mem_sample, norm_mix_g, w_in, na_rpb, conv_w, swa_sink, t5_bias, w_out,
                             norm_xa_g, norm_mem_g, w_xq, w_xk, w_xv, w_xo,
                             norm_ffn_g, peer_wq, peer_subkeys, peer_u, peer_v, final_g)
    return (y_prompt, y_sample)
```

```python
import functools

import numpy as np
import jax
import jax.numpy as jnp
from jax import lax
from jax.experimental import pallas as pl
from jax.experimental.pallas import tpu as pltpu

D_MODEL = 1024
GRID_W = 64
HEAD_DIM = 64
NA_HEADS = 6
NA_WIN_H = 8
NA_WIN_W = 16
NA_DIM = NA_HEADS * HEAD_DIM
SC_DIM = 256
SWA_HEADS = 6
SWA_KV_HEADS = 2
SWA_GROUP = SWA_HEADS // SWA_KV_HEADS
SWA_WINDOW = 128
SWA_DIM = SWA_HEADS * HEAD_DIM
SWA_KV_DIM = SWA_KV_HEADS * HEAD_DIM
T5_BUCKETS = 32
T5_MAX_DIST = 128
IN_DIM = 3 * NA_DIM + 3 * SC_DIM + SWA_DIM + 2 * SWA_KV_DIM
XA_HEADS = 4
XA_HEAD_DIM = 128
XA_DIM = XA_HEADS * XA_HEAD_DIM
PEER_HEADS = 8
PEER_NKEYS = 128
PEER_EXPERTS = PEER_NKEYS * PEER_NKEYS
PEER_TOPK = 16
PEER_HALF = 128
RMS_EPS = 1e-6
NEG_INF = -1e30

LANES = 128
SUBLANES = 8
VMEM_LIMIT = 56 * 1024 * 1024

TOK_TILE = 512
NA_Q_ROWS = 4
NA_K_ROWS = NA_Q_ROWS + NA_WIN_H
NA_Q = NA_Q_ROWS * GRID_W
NA_K = NA_K_ROWS * GRID_W
SWA_Q = 256
SWA_K = SWA_Q + 2 * SWA_WINDOW
PEER_TOK = 512
PEER_EB = 1024
PEER_CHUNKS = PEER_TOK // LANES

_NT = (((1,), (1,)), ((), ()))


def _rms(xf, g):
    return xf * lax.rsqrt(jnp.mean(xf * xf, axis=-1, keepdims=True) + RMS_EPS) * g


def _dot(a, b):
    return jnp.dot(a, b, preferred_element_type=jnp.float32)


def _dot_nt(a, b):
    return lax.dot_general(a, b, _NT, preferred_element_type=jnp.float32)


def _params(sem):
    return pltpu.CompilerParams(dimension_semantics=sem, vmem_limit_bytes=VMEM_LIMIT)


def _mix_in_kernel(x_ref, g_ref, w_ref, naq_ref, nak_ref, nav_ref, u_ref, b_ref, swq_ref, swk_ref, swv_ref):
    xn = _rms(x_ref[...], g_ref[...]).astype(jnp.bfloat16)
    z = _dot(xn, w_ref[...])
    o = 0
    naq_ref[...] = z[:, o:o + NA_DIM].astype(jnp.bfloat16); o += NA_DIM
    nak_ref[...] = z[:, o:o + NA_DIM].astype(jnp.bfloat16); o += NA_DIM
    nav_ref[...] = z[:, o:o + NA_DIM].astype(jnp.bfloat16); o += NA_DIM
    b_ref[...] = z[:, o:o + SC_DIM]; o += SC_DIM
    u_ref[...] = z[:, o:o + SC_DIM] * z[:, o + SC_DIM:o + 2 * SC_DIM]; o += 2 * SC_DIM
    swq_ref[...] = z[:, o:o + SWA_DIM].astype(jnp.bfloat16); o += SWA_DIM
    swk_ref[...] = z[:, o:o + SWA_KV_DIM].astype(jnp.bfloat16); o += SWA_KV_DIM
    swv_ref[...] = z[:, o:o + SWA_KV_DIM].astype(jnp.bfloat16)


def _mix_in(x, g, w):
    b, s, _ = x.shape
    tok = lambda d: pl.BlockSpec((None, TOK_TILE, d), lambda bi, i: (bi, i, 0))
    full = lambda shape: pl.BlockSpec(shape, lambda bi, i: (0,) * len(shape))
    widths = [(NA_DIM, jnp.bfloat16)] * 3 + [(SC_DIM, jnp.float32)] * 2 + \
             [(SWA_DIM, jnp.bfloat16), (SWA_KV_DIM, jnp.bfloat16), (SWA_KV_DIM, jnp.bfloat16)]
    return pl.pallas_call(
        _mix_in_kernel,
        name="mix_in",
        grid=(b, s // TOK_TILE),
        in_specs=[tok(D_MODEL), full((1, D_MODEL)), full((D_MODEL, IN_DIM))],
        out_specs=[tok(d) for d, _ in widths],
        out_shape=[jax.ShapeDtypeStruct((b, s, d), t) for d, t in widths],
        compiler_params=_params(("parallel", "parallel")),
    )(x, g, w)


def _pair_heads(q_pair, k_pair, v_pair, bias_lo, bias_hi, extra_lo=None, extra_hi=None):
    low = lax.broadcasted_iota(jnp.int32, (1, LANES), 1) < HEAD_DIM
    outs = []
    for keep, bias, extra in ((low, bias_lo, extra_lo), (~low, bias_hi, extra_hi)):
        qm = jnp.where(keep, q_pair, jnp.zeros_like(q_pair))
        logits = _dot_nt(qm, k_pair) + bias
        mx = jnp.max(logits, axis=-1, keepdims=True)
        if extra is not None:
            mx = jnp.maximum(mx, extra)
        p = jnp.exp(logits - mx)
        den = jnp.sum(p, axis=-1, keepdims=True)
        if extra is not None:
            den = den + jnp.exp(extra - mx)
        outs.append(_dot(p.astype(jnp.bfloat16), v_pair) / den)
    return jnp.where(low, outs[0], outs[1])


def _na_kernel(q_ref, k_ref, v_ref, bias_ref, o_ref):
    m = pl.program_id(1)
    rows = k_ref.shape[0] // GRID_W
    start_row = jnp.clip(m * NA_Q_ROWS - NA_WIN_H // 2, 0, rows - NA_K_ROWS)
    start = pl.multiple_of(start_row * GRID_W, GRID_W * NA_Q_ROWS)
    kwin = k_ref[pl.ds(start, NA_K), :]
    vwin = v_ref[pl.ds(start, NA_K), :]
    q = q_ref[...]
    for g in range(NA_HEADS // 2):
        sl = slice(g * LANES, (g + 1) * LANES)
        out = _pair_heads(q[:, sl], kwin[:, sl], vwin[:, sl], bias_ref[2 * g], bias_ref[2 * g + 1])
        o_ref[:, sl] = out.astype(o_ref.dtype)


def _na_case(m, nsteps):
    return jnp.where(m == 0, 0, jnp.where(m == nsteps - 1, 2, 1))


def _na_attn(q, k, v, bias):
    b, s, _ = q.shape
    nsteps = s // NA_Q
    return pl.pallas_call(
        _na_kernel,
        name="na_attn",
        grid=(b, nsteps),
        in_specs=[
            pl.BlockSpec((None, NA_Q, NA_DIM), lambda bi, m: (bi, m, 0)),
            pl.BlockSpec((None, s, NA_DIM), lambda bi, m: (bi, 0, 0)),
            pl.BlockSpec((None, s, NA_DIM), lambda bi, m: (bi, 0, 0)),
            pl.BlockSpec((None, NA_HEADS, NA_Q, NA_K), lambda bi, m: (_na_case(m, nsteps), 0, 0, 0)),
        ],
        out_specs=pl.BlockSpec((None, NA_Q, NA_DIM), lambda bi, m: (bi, m, 0)),
        out_shape=jax.ShapeDtypeStruct((b, s, NA_DIM), jnp.bfloat16),
        compiler_params=_params(("parallel", "arbitrary")),
    )(q, k, v, bias)


def _na_bias_tables(rpb, s):
    rows = s // GRID_W
    nsteps = rows // NA_Q_ROWS

    def geometry(m):
        start_row = int(np.clip(m * NA_Q_ROWS - NA_WIN_H // 2, 0, rows - NA_K_ROWS))
        qi = np.arange(NA_Q)
        kj = np.arange(NA_K)
        r = (m * NA_Q_ROWS + qi // GRID_W)[:, None]
        c = (qi % GRID_W)[:, None]
        kr = (start_row + kj // GRID_W)[None, :]
        kc = (kj % GRID_W)[None, :]
        r0 = np.clip(r - NA_WIN_H // 2, 0, rows - NA_WIN_H)
        c0 = np.clip(c - NA_WIN_W // 2, 0, GRID_W - NA_WIN_W)
        valid = (kr >= r0) & (kr < r0 + NA_WIN_H) & (kc >= c0) & (kc < c0 + NA_WIN_W)
        dr = np.clip(kr - r + NA_WIN_H - 1, 0, 2 * NA_WIN_H - 2)
        dc = np.clip(kc - c + NA_WIN_W - 1, 0, 2 * NA_WIN_W - 2)
        return valid, np.broadcast_to(dr, valid.shape), np.broadcast_to(dc, valid.shape)

    interior = geometry(1)
    for m in range(1, nsteps - 1):
        assert all(np.array_equal(a, b_) for a, b_ in zip(geometry(m), interior))
    tabs = []
    for valid, dr, dc in (geometry(0), interior, geometry(nsteps - 1)):
        t = jnp.where(valid[..., None], rpb[dr, dc].astype(jnp.float32), NEG_INF)
        tabs.append(jnp.transpose(t, (2, 0, 1)))
    return jnp.stack(tabs)


def _swa_kernel(sink_ref, q_ref, k_ref, v_ref, bias_ref, o_ref):
    n = pl.program_id(1)
    s = k_ref.shape[0]
    start = pl.multiple_of(jnp.clip(n * SWA_Q - SWA_WINDOW, 0, s - SWA_K), SWA_WINDOW)
    kwin = k_ref[pl.ds(start, SWA_K), :]
    vwin = v_ref[pl.ds(start, SWA_K), :]
    q = q_ref[...]
    for g in range(SWA_GROUP):
        sl = slice(g * LANES, (g + 1) * LANES)
        out = _pair_heads(q[:, sl], kwin, vwin, bias_ref[2 * g], bias_ref[2 * g + 1],
                          sink_ref[g], sink_ref[g + SWA_GROUP])
        o_ref[:, sl] = out.astype(o_ref.dtype)


def _swa_attn(sink, q, k, v, bias):
    b, s, _ = q.shape
    nsteps = s // SWA_Q
    return pl.pallas_call(
        _swa_kernel,
        name="swa_attn",
        grid=(b, nsteps),
        in_specs=[
            pl.BlockSpec(memory_space=pltpu.SMEM),
            pl.BlockSpec((None, SWA_Q, SWA_DIM), lambda bi, n: (bi, n, 0)),
            pl.BlockSpec((None, s, SWA_KV_DIM), lambda bi, n: (bi, 0, 0)),
            pl.BlockSpec((None, s, SWA_KV_DIM), lambda bi, n: (bi, 0, 0)),
            pl.BlockSpec((None, SWA_HEADS, SWA_Q, SWA_K), lambda bi, n: (_na_case(n, nsteps), 0, 0, 0)),
        ],
        out_specs=pl.BlockSpec((None, SWA_Q, SWA_DIM), lambda bi, n: (bi, n, 0)),
        out_shape=jax.ShapeDtypeStruct((b, s, SWA_DIM), jnp.bfloat16),
        compiler_params=_params(("parallel", "arbitrary")),
    )(sink, q, k, v, bias)


def _t5_bucket(rel):
    nb = T5_BUCKETS // 2
    max_exact = nb // 2
    ret = (rel > 0).astype(np.int32) * nb
    n = np.abs(rel)
    large = max_exact + (np.log(np.maximum(n, 1) / max_exact) / np.log(T5_MAX_DIST / max_exact)
                         * (nb - max_exact)).astype(np.int32)
    large = np.minimum(large, nb - 1)
    return (ret + np.where(n < max_exact, n, large)).astype(np.int32)


_SWA_PERM = np.concatenate([
    np.concatenate([np.arange(h * HEAD_DIM, (h + 1) * HEAD_DIM) for h in (g, g + SWA_GROUP)])
    for g in range(SWA_GROUP)])
_SWA_SLOT_HEADS = np.array([h for g in range(SWA_GROUP) for h in (g, g + SWA_GROUP)])


def _swa_bias_tables(t5_bias, s):
    offsets = np.arange(-SWA_WINDOW, SWA_WINDOW + 1)
    bias_off = t5_bias[_t5_bucket(offsets)].astype(jnp.float32)
    nsteps = s // SWA_Q
    tabs = []
    for n in (0, 1, nsteps - 1):
        start = int(np.clip(n * SWA_Q - SWA_WINDOW, 0, s - SWA_K))
        off = (start + np.arange(SWA_K))[None, :] - (n * SWA_Q + np.arange(SWA_Q))[:, None]
        valid = np.abs(off) <= SWA_WINDOW
        t = jnp.where(valid[..., None], bias_off[np.clip(off + SWA_WINDOW, 0, 2 * SWA_WINDOW)], NEG_INF)
        tabs.append(jnp.transpose(t, (2, 0, 1))[_SWA_SLOT_HEADS])
    return jnp.stack(tabs)


def _mem_kv_kernel(mem_ref, g_ref, wk_ref, wv_ref, k_ref, v_ref):
    mn = _rms(mem_ref[...], g_ref[...]).astype(jnp.bfloat16)
    k_ref[...] = _dot(mn, wk_ref[...]).astype(jnp.bfloat16)
    v_ref[...] = _dot(mn, wv_ref[...]).astype(jnp.bfloat16)


def _mem_kv(mem, g, wk, wv):
    b, m, _ = mem.shape
    full = lambda shape: pl.BlockSpec(shape, lambda bi: (0,) * len(shape))
    return pl.pallas_call(
        _mem_kv_kernel,
        name="mem_kv",
        grid=(b,),
        in_specs=[pl.BlockSpec((None, m, D_MODEL), lambda bi: (bi, 0, 0)), full((1, D_MODEL)),
                  full((D_MODEL, XA_DIM)), full((D_MODEL, XA_DIM))],
        out_specs=[pl.BlockSpec((None, m, XA_DIM), lambda bi: (bi, 0, 0))] * 2,
        out_shape=[jax.ShapeDtypeStruct((b, m, XA_DIM), jnp.bfloat16)] * 2,
        compiler_params=_params(("parallel",)),
    )(mem, g, wk, wv)


def _mix_out_kernel(x_ref, yna_ref, ysw_ref, u_ref, uprev_ref, unext_ref, b_ref, cw_ref,
                    wna_ref, wsc_ref, wsw_ref, g_ref, wq_ref, km_ref, vm_ref, wo_ref, o_ref):
    i = pl.program_id(1)
    nt = pl.num_programs(1)
    u = u_ref[...]
    t = u.shape[0]
    row = lax.broadcasted_iota(jnp.int32, (t, 1), 0)
    prev_row = jnp.where(i > 0, uprev_ref[SUBLANES - 1:SUBLANES, :], 0.0)
    next_row = jnp.where(i < nt - 1, unext_ref[0:1, :], 0.0)
    u_m1 = jnp.where(row == 0, prev_row, pltpu.roll(u, 1, axis=0))
    u_p1 = jnp.where(row == t - 1, next_row, pltpu.roll(u, t - 1, axis=0))
    ysc = b_ref[...] * (u_m1 * cw_ref[0:1, :] + u * cw_ref[1:2, :] + u_p1 * cw_ref[2:3, :])
    y = _dot(yna_ref[...], wna_ref[...]) + _dot(ysc.astype(jnp.bfloat16), wsc_ref[...]) \
        + _dot(ysw_ref[...], wsw_ref[...])
    x1 = x_ref[...] + y

    xn = _rms(x1, g_ref[...]).astype(jnp.bfloat16)
    q = _dot(xn, wq_ref[...])
    outs = []
    for h in range(XA_HEADS):
        sl = slice(h * XA_HEAD_DIM, (h + 1) * XA_HEAD_DIM)
        logits = _dot_nt(q[:, sl].astype(jnp.bfloat16), km_ref[:, sl]) * (XA_HEAD_DIM ** -0.5)
        mx = jnp.max(logits, axis=-1, keepdims=True)
        p = jnp.exp(logits - mx)
        den = jnp.sum(p, axis=-1, keepdims=True)
        outs.append((_dot(p.astype(jnp.bfloat16), vm_ref[:, sl]) / den).astype(jnp.bfloat16))
    o = jnp.concatenate(outs, axis=-1)
    o_ref[...] = x1 + _dot(o, wo_ref[...])


def _mix_out(x, yna, ysw, u, bgate, cw, wna, wsc, wsw, g, wq, km, vm, wo):
    b, s, _ = x.shape
    nt = s // TOK_TILE
    halo_blocks = TOK_TILE // SUBLANES
    last_halo = s // SUBLANES - 1
    tok = lambda d: pl.BlockSpec((None, TOK_TILE, d), lambda bi, i: (bi, i, 0))
    full = lambda shape: pl.BlockSpec(shape, lambda bi, i: (0,) * len(shape))
    mem = pl.BlockSpec((None, km.shape[1], XA_DIM), lambda bi, i: (bi, 0, 0))
    return pl.pallas_call(
        _mix_out_kernel,
        name="mix_out",
        grid=(b, nt),
        in_specs=[
            tok(D_MODEL), tok(NA_DIM), tok(SWA_DIM), tok(SC_DIM),
            pl.BlockSpec((None, SUBLANES, SC_DIM), lambda bi, i: (bi, jnp.maximum(i * halo_blocks - 1, 0), 0)),
            pl.BlockSpec((None, SUBLANES, SC_DIM), lambda bi, i: (bi, jnp.minimum((i + 1) * halo_blocks, last_halo), 0)),
            tok(SC_DIM), full((3, SC_DIM)),
            full((NA_DIM, D_MODEL)), full((SC_DIM, D_MODEL)), full((SWA_DIM, D_MODEL)),
            full((1, D_MODEL)), full((D_MODEL, XA_DIM)), mem, mem, full((XA_DIM, D_MODEL)),
        ],
        out_specs=tok(D_MODEL),
        out_shape=jax.ShapeDtypeStruct((b, s, D_MODEL), jnp.float32),
        compiler_params=_params(("parallel", "parallel")),
    )(x, yna, ysw, u, u, u, bgate, cw, wna, wsc, wsw, g, wq, km, vm, wo)


_CAND_PAIRS = [(p, q) for p in range(PEER_TOPK) for q in range(PEER_TOPK) if (p + 1) * (q + 1) <= PEER_TOPK]
_CAND_VREGS = -(-len(_CAND_PAIRS) // SUBLANES)


def _allmax_sublanes(v):
    for shift in (4, 2, 1):
        v = jnp.maximum(v, pltpu.roll(v, shift, axis=0))
    return v


def _top_values(s3, count):
    vals = []
    for r in range(count):
        m = _allmax_sublanes(jnp.max(s3, axis=0))
        vals.append(m)
        if r + 1 < count:
            s3 = jnp.where(s3 == m[None], -jnp.inf, s3)
    return vals


def _peer_prep_chunk(s0, s1):
    groups = PEER_NKEYS // SUBLANES
    a = _top_values(s0.reshape(groups, SUBLANES, LANES), PEER_TOPK)
    b = _top_values(s1.reshape(groups, SUBLANES, LANES), PEER_TOPK)
    sub = lax.broadcasted_iota(jnp.int32, (SUBLANES, LANES), 0)
    cand = []
    for v in range(_CAND_VREGS):
        acc = jnp.full((SUBLANES, LANES), -jnp.inf, jnp.float32)
        for k, (p, q) in enumerate(_CAND_PAIRS[v * SUBLANES:(v + 1) * SUBLANES]):
            acc = jnp.where(sub == k, a[p] + b[q], acc)
        cand.append(acc)
    best = _top_values(jnp.stack(cand), PEER_TOPK)
    z = jnp.ones((SUBLANES, LANES), jnp.float32)
    for r in range(1, PEER_TOPK):
        z = z + jnp.exp(best[r] - best[0])
    tau = best[PEER_TOPK - 1][0:1, :]
    theta = tau - s0
    rowgate = jnp.exp(s0 - a[0][0:1, :]) / z[0:1, :]
    colgate = jnp.exp(s1 - b[0][0:1, :])
    return theta, rowgate, colgate


def _peer_kernel(x_ref, g_ref, wqt_ref, sk_ref, u_ref, vt_ref, fg_ref, o_ref,
                 xnt_ref, theta_ref, rowg_ref, s1_ref, colg_ref, h_ref, gt_ref, acc_ref, *, final_norm):
    eb = pl.program_id(2)
    neb = pl.num_programs(2)

    @pl.when(eb == 0)
    def _prep():
        xn = _rms(x_ref[...], g_ref[...])
        xnt_ref[...] = xn.T.astype(jnp.bfloat16)

        def head(h, carry):
            halves = []
            for p in range(2):
                hp = 2 * h + p
                qhp = _dot(wqt_ref[pl.ds(pl.multiple_of(hp * PEER_HALF, PEER_HALF), PEER_HALF), :], xnt_ref[...])
                halves.append(_dot(sk_ref[hp], qhp.astype(jnp.bfloat16)))
            for c in range(PEER_CHUNKS):
                sl = slice(c * LANES, (c + 1) * LANES)
                theta, rowg, colg = _peer_prep_chunk(halves[0][:, sl], halves[1][:, sl])
                theta_ref[h, c] = theta
                rowg_ref[h, c] = rowg
                s1_ref[h, c] = halves[1][:, sl]
                colg_ref[h, c] = colg
            return carry

        lax.fori_loop(0, PEER_HEADS, head, 0)

    h_ref[...] = _dot(u_ref[...], xnt_ref[...])

    def row_block(ii, carry):
        i_glob = eb * (PEER_EB // PEER_NKEYS) + ii
        r0 = pl.multiple_of(ii * PEER_NKEYS, PEER_NKEYS)
        for c in range(PEER_CHUNKS):
            sl = slice(c * LANES, (c + 1) * LANES)
            w = jnp.zeros((PEER_NKEYS, LANES), jnp.float32)
            for h in range(PEER_HEADS):
                th = theta_ref[h, c, pl.ds(i_glob, 1), :]
                rg = rowg_ref[h, c, pl.ds(i_glob, 1), :]
                w = w + jnp.where(s1_ref[h, c] >= th, colg_ref[h, c], 0.0) * rg
            hh = h_ref[pl.ds(r0, PEER_NKEYS), sl]
            act = 0.5 * hh * (1.0 + lax.erf(hh * (2.0 ** -0.5)))
            gt_ref[pl.ds(r0, PEER_NKEYS), sl] = (w * act).astype(jnp.bfloat16)
        return carry

    lax.fori_loop(0, PEER_EB // PEER_NKEYS, row_block, 0)

    contrib = _dot(vt_ref[...], gt_ref[...])

    @pl.when(eb == 0)
    def _init():
        acc_ref[...] = contrib

    @pl.when(eb > 0)
    def _accum():
        acc_ref[...] += contrib

    @pl.when(eb == neb - 1)
    def _finish():
        y = x_ref[...] + acc_ref[...].T
        if final_norm:
            y = _rms(y, fg_ref[...])
        o_ref[...] = y


def _peer(x, g, wqt, sk, u, vt, fg, final_norm):
    b, s, _ = x.shape
    tok = pl.BlockSpec((None, PEER_TOK, D_MODEL), lambda bi, i, e: (bi, i, 0))
    full = lambda shape: pl.BlockSpec(shape, lambda bi, i, e: (0,) * len(shape))
    gate_scratch = pltpu.VMEM((PEER_HEADS, PEER_CHUNKS, PEER_NKEYS, LANES), jnp.float32)
    return pl.pallas_call(
        functools.partial(_peer_kernel, final_norm=final_norm),
        name="peer",
        grid=(b, s // PEER_TOK, PEER_EXPERTS // PEER_EB),
        in_specs=[
            tok, full((1, D_MODEL)), full((2 * PEER_HEADS * PEER_HALF, D_MODEL)),
            full((2 * PEER_HEADS, PEER_NKEYS, PEER_HALF)),
            pl.BlockSpec((PEER_EB, D_MODEL), lambda bi, i, e: (e, 0)),
            pl.BlockSpec((D_MODEL, PEER_EB), lambda bi, i, e: (0, e)),
            full((1, D_MODEL)),
        ],
        out_specs=tok,
        out_shape=jax.ShapeDtypeStruct((b, s, D_MODEL), jnp.float32),
        scratch_shapes=[
            pltpu.VMEM((D_MODEL, PEER_TOK), jnp.bfloat16),
            gate_scratch, gate_scratch, gate_scratch, gate_scratch,
            pltpu.VMEM((PEER_EB, PEER_TOK), jnp.float32),
            pltpu.VMEM((PEER_EB, PEER_TOK), jnp.bfloat16),
            pltpu.VMEM((D_MODEL, PEER_TOK), jnp.float32),
        ],
        compiler_params=_params(("parallel", "parallel", "arbitrary")),
    )(x, g, wqt, sk, u, vt, fg)


def _prepare_layer(l, w_in, conv_w, swa_sink, w_out, norm_mix_g, norm_xa_g, norm_mem_g, w_xq, w_xk, w_xv, w_xo,
                   norm_ffn_g, peer_wq, peer_subkeys, peer_u, peer_v):
    bf = jnp.bfloat16
    scale = HEAD_DIM ** -0.5
    o = np.cumsum([0, NA_DIM, NA_DIM, NA_DIM, SC_DIM, SC_DIM, SC_DIM, SWA_DIM, SWA_KV_DIM, SWA_KV_DIM])
    wi = w_in[l]
    w_in_l = jnp.concatenate([
        wi[:, o[0]:o[1]] * scale, wi[:, o[1]:o[6]], (wi[:, o[6]:o[7]] * scale)[:, _SWA_PERM], wi[:, o[7]:o[9]],
    ], axis=1).astype(bf)
    wo = w_out[l]
    row = lambda v: v.reshape(1, -1).astype(jnp.float32)
    return dict(
        g_mix=row(norm_mix_g[l]), w_in=w_in_l, conv_w=conv_w[l].astype(jnp.float32),
        sink=swa_sink[l].astype(jnp.float32),
        wna=wo[:NA_DIM].astype(bf), wsc=wo[NA_DIM:NA_DIM + SC_DIM].astype(bf),
        wsw=wo[NA_DIM + SC_DIM:][_SWA_PERM].astype(bf),
        g_xa=row(norm_xa_g[l]), g_mem=row(norm_mem_g[l]),
        wxq=w_xq[l].astype(bf), wxk=w_xk[l].astype(bf), wxv=w_xv[l].astype(bf), wxo=w_xo[l].astype(bf),
        g_ffn=row(norm_ffn_g[l]), wqt=peer_wq[l].T.astype(bf),
        sk=peer_subkeys[l].reshape(2 * PEER_HEADS, PEER_NKEYS, PEER_HALF).astype(bf),
        u=peer_u[l].astype(bf), vt=peer_v[l].T.astype(bf),
    )


def _trunk(x, mem, layers, na_bias, swa_bias, final_g):
    depth = len(layers)
    for l, p in enumerate(layers):
        naq, nak, nav, u, bgate, swq, swk, swv = _mix_in(x, p["g_mix"], p["w_in"])
        yna = _na_attn(naq, nak, nav, na_bias[l])
        ysw = _swa_attn(p["sink"], swq, swk, swv, swa_bias)
        km, vm = _mem_kv(mem, p["g_mem"], p["wxk"], p["wxv"])
        x = _mix_out(x, yna, ysw, u, bgate, p["conv_w"], p["wna"], p["wsc"], p["wsw"],
                     p["g_xa"], p["wxq"], km, vm, p["wxo"])
        x = _peer(x, p["g_ffn"], p["wqt"], p["sk"], p["u"], p["vt"], final_g, final_norm=(l == depth - 1))
    return x


def kernel(x_prompt, x_sample, mem_prompt, mem_sample, norm_mix_g, w_in, na_rpb, conv_w, swa_sink, t5_bias, w_out,
           norm_xa_g, norm_mem_g, w_xq, w_xk, w_xv, w_xo, norm_ffn_g, peer_wq, peer_subkeys, peer_u, peer_v, final_g):
    depth = w_in.shape[0]
    layers = [_prepare_layer(l, w_in, conv_w, swa_sink, w_out, norm_mix_g, norm_xa_g, norm_mem_g, w_xq, w_xk, w_xv,
                             w_xo, norm_ffn_g, peer_wq, peer_subkeys, peer_u, peer_v) for l in range(depth)]
    fg = final_g.reshape(1, -1).astype(jnp.float32)
    outs = []
    for x, mem in ((x_prompt, mem_prompt), (x_sample, mem_sample)):
        s = x.shape[1]
        na_bias = [_na_bias_tables(na_rpb[l], s) for l in range(depth)]
        swa_bias = _swa_bias_tables(t5_bias, s)
        outs.append(_trunk(x, mem, layers, na_bias, swa_bias, fg))
    return tuple(outs)
```

```python
import functools

import numpy as np
import jax
import jax.numpy as jnp
from jax import lax
from jax.experimental import pallas as pl
from jax.experimental.pallas import tpu as pltpu

D_MODEL = 1024
GRID_W = 64
HEAD_DIM = 64
NA_HEADS = 6
NA_WIN_H = 8
NA_WIN_W = 16
NA_DIM = NA_HEADS * HEAD_DIM
SC_DIM = 256
SWA_HEADS = 6
SWA_KV_HEADS = 2
SWA_GROUP = SWA_HEADS // SWA_KV_HEADS
SWA_WINDOW = 128
SWA_DIM = SWA_HEADS * HEAD_DIM
SWA_KV_DIM = SWA_KV_HEADS * HEAD_DIM
T5_BUCKETS = 32
T5_MAX_DIST = 128
IN_DIM = 3 * NA_DIM + 3 * SC_DIM + SWA_DIM + 2 * SWA_KV_DIM
XA_HEADS = 4
XA_HEAD_DIM = 128
XA_DIM = XA_HEADS * XA_HEAD_DIM
PEER_HEADS = 8
PEER_NKEYS = 128
PEER_EXPERTS = PEER_NKEYS * PEER_NKEYS
PEER_TOPK = 16
PEER_HALF = 128
RMS_EPS = 1e-6
NEG_INF = -1e30

LANES = 128
SUBLANES = 8
VMEM_LIMIT = 56 * 1024 * 1024

TOK_TILE = 512
NA_Q_ROWS = 4
NA_K_ROWS = NA_Q_ROWS + NA_WIN_H
NA_Q = NA_Q_ROWS * GRID_W
NA_K = NA_K_ROWS * GRID_W
SWA_Q = 256
SWA_K = SWA_Q + 2 * SWA_WINDOW
PEER_TOK = 512
PEER_EB = 1024
PEER_CHUNKS = PEER_TOK // LANES

_NT = (((1,), (1,)), ((), ()))


def _rms(xf, g):
    return xf * lax.rsqrt(jnp.mean(xf * xf, axis=-1, keepdims=True) + RMS_EPS) * g


def _dot(a, b):
    return jnp.dot(a, b, preferred_element_type=jnp.float32)


def _dot_nt(a, b):
    return lax.dot_general(a, b, _NT, preferred_element_type=jnp.float32)


def _params(sem, flags=None):
    return pltpu.CompilerParams(dimension_semantics=sem, vmem_limit_bytes=VMEM_LIMIT, flags=flags)


def _mix_in_kernel(x_ref, g_ref, w_ref, naq_ref, nak_ref, nav_ref, u_ref, b_ref, swq_ref, swk_ref, swv_ref):
    xn = _rms(x_ref[...], g_ref[...]).astype(jnp.bfloat16)
    z = _dot(xn, w_ref[...])
    o = 0
    naq_ref[...] = z[:, o:o + NA_DIM].astype(jnp.bfloat16); o += NA_DIM
    nak_ref[...] = z[:, o:o + NA_DIM].astype(jnp.bfloat16); o += NA_DIM
    nav_ref[...] = z[:, o:o + NA_DIM].astype(jnp.bfloat16); o += NA_DIM
    b_ref[...] = z[:, o:o + SC_DIM]; o += SC_DIM
    u_ref[...] = z[:, o:o + SC_DIM] * z[:, o + SC_DIM:o + 2 * SC_DIM]; o += 2 * SC_DIM
    swq_ref[...] = z[:, o:o + SWA_DIM].astype(jnp.bfloat16); o += SWA_DIM
    swk_ref[...] = z[:, o:o + SWA_KV_DIM].astype(jnp.bfloat16); o += SWA_KV_DIM
    swv_ref[...] = z[:, o:o + SWA_KV_DIM].astype(jnp.bfloat16)


def _mix_in(x, g, w):
    b, s, _ = x.shape
    tok = lambda d: pl.BlockSpec((None, TOK_TILE, d), lambda bi, i: (bi, i, 0))
    full = lambda shape: pl.BlockSpec(shape, lambda bi, i: (0,) * len(shape))
    widths = [(NA_DIM, jnp.bfloat16)] * 3 + [(SC_DIM, jnp.float32)] * 2 + \
             [(SWA_DIM, jnp.bfloat16), (SWA_KV_DIM, jnp.bfloat16), (SWA_KV_DIM, jnp.bfloat16)]
    return pl.pallas_call(
        _mix_in_kernel,
        name="mix_in",
        grid=(b, s // TOK_TILE),
        in_specs=[tok(D_MODEL), full((1, D_MODEL)), full((D_MODEL, IN_DIM))],
        out_specs=[tok(d) for d, _ in widths],
        out_shape=[jax.ShapeDtypeStruct((b, s, d), t) for d, t in widths],
        compiler_params=_params(("parallel", "parallel")),
    )(x, g, w)


def _pair_heads(q_pair, k_pair, v_pair, bias_lo, bias_hi, extra_lo=None, extra_hi=None):
    low = lax.broadcasted_iota(jnp.int32, (1, LANES), 1) < HEAD_DIM
    outs = []
    for keep, bias, extra in ((low, bias_lo, extra_lo), (~low, bias_hi, extra_hi)):
        qm = jnp.where(keep, q_pair, jnp.zeros_like(q_pair))
        logits = _dot_nt(qm, k_pair) + bias
        mx = jnp.max(logits, axis=-1, keepdims=True)
        if extra is not None:
            mx = jnp.maximum(mx, extra)
        p = jnp.exp(logits - mx)
        den = jnp.sum(p, axis=-1, keepdims=True)
        if extra is not None:
            den = den + jnp.exp(extra - mx)
        outs.append(_dot(p.astype(jnp.bfloat16), v_pair) / den)
    return jnp.where(low, outs[0], outs[1])


def _na_kernel(q_ref, k_ref, v_ref, bias_ref, o_ref):
    m = pl.program_id(1)
    rows = k_ref.shape[0] // GRID_W
    start_row = jnp.clip(m * NA_Q_ROWS - NA_WIN_H // 2, 0, rows - NA_K_ROWS)
    start = pl.multiple_of(start_row * GRID_W, GRID_W * NA_Q_ROWS)
    kwin = k_ref[pl.ds(start, NA_K), :]
    vwin = v_ref[pl.ds(start, NA_K), :]
    q = q_ref[...]
    for g in range(NA_HEADS // 2):
        sl = slice(g * LANES, (g + 1) * LANES)
        out = _pair_heads(q[:, sl], kwin[:, sl], vwin[:, sl], bias_ref[2 * g], bias_ref[2 * g + 1])
        o_ref[:, sl] = out.astype(o_ref.dtype)


def _na_case(m, nsteps):
    return jnp.where(m == 0, 0, jnp.where(m == nsteps - 1, 2, 1))


def _na_attn(q, k, v, bias):
    b, s, _ = q.shape
    nsteps = s // NA_Q
    return pl.pallas_call(
        _na_kernel,
        name="na_attn",
        grid=(b, nsteps),
        in_specs=[
            pl.BlockSpec((None, NA_Q, NA_DIM), lambda bi, m: (bi, m, 0)),
            pl.BlockSpec((None, s, NA_DIM), lambda bi, m: (bi, 0, 0)),
            pl.BlockSpec((None, s, NA_DIM), lambda bi, m: (bi, 0, 0)),
            pl.BlockSpec((None, NA_HEADS, NA_Q, NA_K), lambda bi, m: (_na_case(m, nsteps), 0, 0, 0)),
        ],
        out_specs=pl.BlockSpec((None, NA_Q, NA_DIM), lambda bi, m: (bi, m, 0)),
        out_shape=jax.ShapeDtypeStruct((b, s, NA_DIM), jnp.bfloat16),
        compiler_params=_params(("parallel", "arbitrary")),
    )(q, k, v, bias)


def _na_bias_tables(rpb, s):
    rows = s // GRID_W
    nsteps = rows // NA_Q_ROWS

    c = np.arange(GRID_W)[:, None]
    kc = np.arange(GRID_W)[None, :]
    c0 = np.clip(c - NA_WIN_W // 2, 0, GRID_W - NA_WIN_W)
    valid_c = (kc >= c0) & (kc < c0 + NA_WIN_W)
    sel_c = (kc - c + NA_WIN_W - 1)[..., None] == np.arange(2 * NA_WIN_W - 1)

    def row_geometry(m):
        start_row = int(np.clip(m * NA_Q_ROWS - NA_WIN_H // 2, 0, rows - NA_K_ROWS))
        r = (m * NA_Q_ROWS + np.arange(NA_Q_ROWS))[:, None]
        kr = (start_row + np.arange(NA_K_ROWS))[None, :]
        r0 = np.clip(r - NA_WIN_H // 2, 0, rows - NA_WIN_H)
        valid_r = (kr >= r0) & (kr < r0 + NA_WIN_H)
        sel_r = (kr - r + NA_WIN_H - 1)[..., None] == np.arange(2 * NA_WIN_H - 1)
        return valid_r, sel_r

    interior = row_geometry(1)
    for m in range(1, nsteps - 1):
        assert all(np.array_equal(a, b_) for a, b_ in zip(row_geometry(m), interior))
    tabs = []
    for valid_r, sel_r in (row_geometry(0), interior, row_geometry(nsteps - 1)):
        t = jnp.einsum("rka,abh,cjb->hrckj", sel_r.astype(np.float32), rpb.astype(jnp.float32),
                       sel_c.astype(np.float32), precision=lax.Precision.HIGHEST)
        valid = valid_r[:, None, :, None] & valid_c[None, :, None, :]
        tabs.append(jnp.where(valid[None], t, NEG_INF).reshape(NA_HEADS, NA_Q, NA_K))
    return jnp.stack(tabs)


def _swa_kernel(sink_ref, q_ref, k_ref, v_ref, bias_ref, o_ref):
    n = pl.program_id(1)
    s = k_ref.shape[0]
    start = pl.multiple_of(jnp.clip(n * SWA_Q - SWA_WINDOW, 0, s - SWA_K), SWA_WINDOW)
    kwin = k_ref[pl.ds(start, SWA_K), :]
    vwin = v_ref[pl.ds(start, SWA_K), :]
    q = q_ref[...]
    for g in range(SWA_GROUP):
        sl = slice(g * LANES, (g + 1) * LANES)
        out = _pair_heads(q[:, sl], kwin, vwin, bias_ref[2 * g], bias_ref[2 * g + 1],
                          sink_ref[g], sink_ref[g + SWA_GROUP])
        o_ref[:, sl] = out.astype(o_ref.dtype)


def _swa_attn(sink, q, k, v, bias):
    b, s, _ = q.shape
    nsteps = s // SWA_Q
    return pl.pallas_call(
        _swa_kernel,
        name="swa_attn",
        grid=(b, nsteps),
        in_specs=[
            pl.BlockSpec(memory_space=pltpu.SMEM),
            pl.BlockSpec((None, SWA_Q, SWA_DIM), lambda bi, n: (bi, n, 0)),
            pl.BlockSpec((None, s, SWA_KV_DIM), lambda bi, n: (bi, 0, 0)),
            pl.BlockSpec((None, s, SWA_KV_DIM), lambda bi, n: (bi, 0, 0)),
            pl.BlockSpec((None, SWA_HEADS, SWA_Q, SWA_K), lambda bi, n: (_na_case(n, nsteps), 0, 0, 0)),
        ],
        out_specs=pl.BlockSpec((None, SWA_Q, SWA_DIM), lambda bi, n: (bi, n, 0)),
        out_shape=jax.ShapeDtypeStruct((b, s, SWA_DIM), jnp.bfloat16),
        compiler_params=_params(("parallel", "arbitrary")),
    )(sink, q, k, v, bias)


def _t5_bucket(rel):
    nb = T5_BUCKETS // 2
    max_exact = nb // 2
    ret = (rel > 0).astype(np.int32) * nb
    n = np.abs(rel)
    large = max_exact + (np.log(np.maximum(n, 1) / max_exact) / np.log(T5_MAX_DIST / max_exact)
                         * (nb - max_exact)).astype(np.int32)
    large = np.minimum(large, nb - 1)
    return (ret + np.where(n < max_exact, n, large)).astype(np.int32)


_SWA_SLOT_HEADS = [h for g in range(SWA_GROUP) for h in (g, g + SWA_GROUP)]


def _swa_regroup(w, axis):
    blocks = [lax.slice_in_dim(w, h * HEAD_DIM, (h + 1) * HEAD_DIM, axis=axis) for h in _SWA_SLOT_HEADS]
    return jnp.concatenate(blocks, axis=axis)


def _swa_bias_tables(t5_bias, s):
    offsets = np.arange(-SWA_WINDOW, SWA_WINDOW + 1)
    sel = _t5_bucket(offsets)[:, None] == np.arange(T5_BUCKETS)
    bias_off = jnp.dot(sel.astype(np.float32), t5_bias.astype(jnp.float32), precision=lax.Precision.HIGHEST)
    bias_off = jnp.stack([bias_off[:, h] for h in _SWA_SLOT_HEADS])
    nsteps = s // SWA_Q
    period = SWA_Q + SWA_K - 1
    tabs = []
    for n in (0, 1, nsteps - 1):
        start = int(np.clip(n * SWA_Q - SWA_WINDOW, 0, s - SWA_K))
        delta = n * SWA_Q - start
        lead = SWA_Q - 1 + delta - SWA_WINDOW
        g = jnp.concatenate([jnp.full((SWA_HEADS, lead), NEG_INF, jnp.float32), bias_off,
                             jnp.full((SWA_HEADS, period - lead - 2 * SWA_WINDOW - 1), NEG_INF, jnp.float32)], axis=1)
        hankel = jnp.tile(g, (1, SWA_Q + 1))[:, :SWA_Q * (period + 1)].reshape(SWA_HEADS, SWA_Q, period + 1)
        tabs.append(jnp.flip(hankel[:, :, :SWA_K], axis=1))
    return jnp.stack(tabs)


def _mem_kv_kernel(mem_ref, g_ref, wk_ref, wv_ref, k_ref, v_ref):
    mn = _rms(mem_ref[...], g_ref[...]).astype(jnp.bfloat16)
    k_ref[...] = _dot(mn, wk_ref[...]).astype(jnp.bfloat16)
    v_ref[...] = _dot(mn, wv_ref[...]).astype(jnp.bfloat16)


def _mem_kv(mem, g, wk, wv):
    b, m, _ = mem.shape
    full = lambda shape: pl.BlockSpec(shape, lambda bi: (0,) * len(shape))
    return pl.pallas_call(
        _mem_kv_kernel,
        name="mem_kv",
        grid=(b,),
        in_specs=[pl.BlockSpec((None, m, D_MODEL), lambda bi: (bi, 0, 0)), full((1, D_MODEL)),
                  full((D_MODEL, XA_DIM)), full((D_MODEL, XA_DIM))],
        out_specs=[pl.BlockSpec((None, m, XA_DIM), lambda bi: (bi, 0, 0))] * 2,
        out_shape=[jax.ShapeDtypeStruct((b, m, XA_DIM), jnp.bfloat16)] * 2,
        compiler_params=_params(("parallel",)),
    )(mem, g, wk, wv)


def _mix_out_kernel(x_ref, yna_ref, ysw_ref, u_ref, uprev_ref, unext_ref, b_ref, cw_ref,
                    wna_ref, wsc_ref, wsw_ref, g_ref, wq_ref, km_ref, vm_ref, wo_ref, o_ref):
    i = pl.program_id(1)
    nt = pl.num_programs(1)
    u = u_ref[...]
    t = u.shape[0]
    row = lax.broadcasted_iota(jnp.int32, (t, 1), 0)
    prev_row = jnp.where(i > 0, uprev_ref[SUBLANES - 1:SUBLANES, :], 0.0)
    next_row = jnp.where(i < nt - 1, unext_ref[0:1, :], 0.0)
    u_m1 = jnp.where(row == 0, prev_row, pltpu.roll(u, 1, axis=0))
    u_p1 = jnp.where(row == t - 1, next_row, pltpu.roll(u, t - 1, axis=0))
    ysc = b_ref[...] * (u_m1 * cw_ref[0:1, :] + u * cw_ref[1:2, :] + u_p1 * cw_ref[2:3, :])
    y = _dot(yna_ref[...], wna_ref[...]) + _dot(ysc.astype(jnp.bfloat16), wsc_ref[...]) \
        + _dot(ysw_ref[...], wsw_ref[...])
    x1 = x_ref[...] + y

    xn = _rms(x1, g_ref[...]).astype(jnp.bfloat16)
    q = _dot(xn, wq_ref[...])
    outs = []
    for h in range(XA_HEADS):
        sl = slice(h * XA_HEAD_DIM, (h + 1) * XA_HEAD_DIM)
        logits = _dot_nt(q[:, sl].astype(jnp.bfloat16), km_ref[:, sl]) * (XA_HEAD_DIM ** -0.5)
        mx = jnp.max(logits, axis=-1, keepdims=True)
        p = jnp.exp(logits - mx)
        den = jnp.sum(p, axis=-1, keepdims=True)
        outs.append((_dot(p.astype(jnp.bfloat16), vm_ref[:, sl]) / den).astype(jnp.bfloat16))
    o = jnp.concatenate(outs, axis=-1)
    o_ref[...] = x1 + _dot(o, wo_ref[...])


def _mix_out(x, yna, ysw, u, bgate, cw, wna, wsc, wsw, g, wq, km, vm, wo):
    b, s, _ = x.shape
    nt = s // TOK_TILE
    halo_blocks = TOK_TILE // SUBLANES
    last_halo = s // SUBLANES - 1
    tok = lambda d: pl.BlockSpec((None, TOK_TILE, d), lambda bi, i: (bi, i, 0))
    full = lambda shape: pl.BlockSpec(shape, lambda bi, i: (0,) * len(shape))
    mem = pl.BlockSpec((None, km.shape[1], XA_DIM), lambda bi, i: (bi, 0, 0))
    return pl.pallas_call(
        _mix_out_kernel,
        name="mix_out",
        grid=(b, nt),
        in_specs=[
            tok(D_MODEL), tok(NA_DIM), tok(SWA_DIM), tok(SC_DIM),
            pl.BlockSpec((None, SUBLANES, SC_DIM), lambda bi, i: (bi, jnp.maximum(i * halo_blocks - 1, 0), 0)),
            pl.BlockSpec((None, SUBLANES, SC_DIM), lambda bi, i: (bi, jnp.minimum((i + 1) * halo_blocks, last_halo), 0)),
            tok(SC_DIM), full((3, SC_DIM)),
            full((NA_DIM, D_MODEL)), full((SC_DIM, D_MODEL)), full((SWA_DIM, D_MODEL)),
            full((1, D_MODEL)), full((D_MODEL, XA_DIM)), mem, mem, full((XA_DIM, D_MODEL)),
        ],
        out_specs=tok(D_MODEL),
        out_shape=jax.ShapeDtypeStruct((b, s, D_MODEL), jnp.float32),
        compiler_params=_params(("parallel", "parallel")),
    )(x, yna, ysw, u, u, u, bgate, cw, wna, wsc, wsw, g, wq, km, vm, wo)


_CAND_PAIRS = [(p, q) for p in range(PEER_TOPK) for q in range(PEER_TOPK) if (p + 1) * (q + 1) <= PEER_TOPK]
_CAND_VREGS = -(-len(_CAND_PAIRS) // SUBLANES)


def _allmax_sublanes(v):
    for shift in (4, 2, 1):
        v = jnp.maximum(v, pltpu.roll(v, shift, axis=0))
    return v


def _top_values(s3, count):
    vals = []
    for r in range(count):
        m = _allmax_sublanes(jnp.max(s3, axis=0))
        vals.append(m)
        if r + 1 < count:
            s3 = jnp.where(s3 == m[None], -jnp.inf, s3)
    return vals


def _peer_prep_chunk(s0, s1):
    groups = PEER_NKEYS // SUBLANES
    a = _top_values(s0.reshape(groups, SUBLANES, LANES), PEER_TOPK)
    b = _top_values(s1.reshape(groups, SUBLANES, LANES), PEER_TOPK)
    sub = lax.broadcasted_iota(jnp.int32, (SUBLANES, LANES), 0)
    cand = []
    for v in range(_CAND_VREGS):
        acc = jnp.full((SUBLANES, LANES), -jnp.inf, jnp.float32)
        for k, (p, q) in enumerate(_CAND_PAIRS[v * SUBLANES:(v + 1) * SUBLANES]):
            acc = jnp.where(sub == k, a[p] + b[q], acc)
        cand.append(acc)
    best = _top_values(jnp.stack(cand), PEER_TOPK)
    z = jnp.ones((SUBLANES, LANES), jnp.float32)
    for r in range(1, PEER_TOPK):
        z = z + jnp.exp(best[r] - best[0])
    tau = best[PEER_TOPK - 1][0:1, :]
    theta = tau - s0
    rowgate = jnp.exp(s0 - a[0][0:1, :]) / z[0:1, :]
    colgate = jnp.exp(s1 - b[0][0:1, :])
    return theta, rowgate, colgate


def _peer_gate_rows(eb, ii, theta_ref, rowg_ref, s1_ref, colg_ref, h_ref, gt_ref):
    i_glob = eb * (PEER_EB // PEER_NKEYS) + ii
    rows = pl.ds(pl.multiple_of(ii * PEER_NKEYS, PEER_NKEYS), PEER_NKEYS)
    for c in range(PEER_CHUNKS):
        sl = slice(c * LANES, (c + 1) * LANES)
        w = jnp.zeros((PEER_NKEYS, LANES), jnp.float32)
        for h in range(PEER_HEADS):
            th = theta_ref[h, c, pl.ds(i_glob, 1), :]
            rg = rowg_ref[h, c, pl.ds(i_glob, 1), :]
            w = w + jnp.where(s1_ref[h, c] >= th, colg_ref[h, c], 0.0) * rg
        hh = h_ref[rows, sl]
        act = 0.5 * hh * (1.0 + lax.erf(hh * (2.0 ** -0.5)))
        gt_ref[rows, sl] = (w * act).astype(jnp.bfloat16)


def _peer_kernel(x_ref, g_ref, wqt_ref, sk_ref, u_ref, vt_ref, fg_ref, o_ref,
                 xnt_ref, theta_ref, rowg_ref, s1_ref, colg_ref, h_ref, gt_ref, acc_ref, *, final_norm):
    eb = pl.program_id(2)
    neb = pl.num_programs(2)

    @pl.when(eb == 0)
    def _prep():
        xn = _rms(x_ref[...], g_ref[...])
        xnt_ref[...] = xn.T.astype(jnp.bfloat16)

        def head(h, carry):
            halves = []
            for p in range(2):
                hp = 2 * h + p
                qhp = _dot(wqt_ref[pl.ds(pl.multiple_of(hp * PEER_HALF, PEER_HALF), PEER_HALF), :], xnt_ref[...])
                halves.append(_dot(sk_ref[hp], qhp.astype(jnp.bfloat16)))
            for c in range(PEER_CHUNKS):
                sl = slice(c * LANES, (c + 1) * LANES)
                theta, rowg, colg = _peer_prep_chunk(halves[0][:, sl], halves[1][:, sl])
                theta_ref[h, c] = theta
                rowg_ref[h, c] = rowg
                s1_ref[h, c] = halves[1][:, sl]
                colg_ref[h, c] = colg
            return carry

        lax.fori_loop(0, PEER_HEADS, head, 0)

    h_ref[...] = _dot(u_ref[...], xnt_ref[...])

    def row_block(ii, carry):
        _peer_gate_rows(eb, ii, theta_ref, rowg_ref, s1_ref, colg_ref, h_ref, gt_ref)
        return carry

    lax.fori_loop(0, PEER_EB // PEER_NKEYS, row_block, 0)

    contrib = _dot(vt_ref[...], gt_ref[...])

    @pl.when(eb == 0)
    def _init():
        acc_ref[...] = contrib

    @pl.when(eb > 0)
    def _accum():
        acc_ref[...] += contrib

    @pl.when(eb == neb - 1)
    def _finish():
        y = x_ref[...] + acc_ref[...].T
        if final_norm:
            y = _rms(y, fg_ref[...])
        o_ref[...] = y


def _peer(x, g, wqt, sk, u, vt, fg, final_norm):
    b, s, _ = x.shape
    tok = pl.BlockSpec((None, PEER_TOK, D_MODEL), lambda bi, i, e: (bi, i, 0))
    full = lambda shape: pl.BlockSpec(shape, lambda bi, i, e: (0,) * len(shape))
    gate_scratch = pltpu.VMEM((PEER_HEADS, PEER_CHUNKS, PEER_NKEYS, LANES), jnp.float32)
    return pl.pallas_call(
        functools.partial(_peer_kernel, final_norm=final_norm),
        name="peer",
        grid=(b, s // PEER_TOK, PEER_EXPERTS // PEER_EB),
        in_specs=[
            tok, full((1, D_MODEL)), full((2 * PEER_HEADS * PEER_HALF, D_MODEL)),
            full((2 * PEER_HEADS, PEER_NKEYS, PEER_HALF)),
            pl.BlockSpec((PEER_EB, D_MODEL), lambda bi, i, e: (e, 0)),
            pl.BlockSpec((D_MODEL, PEER_EB), lambda bi, i, e: (0, e)),
            full((1, D_MODEL)),
        ],
        out_specs=tok,
        out_shape=jax.ShapeDtypeStruct((b, s, D_MODEL), jnp.float32),
        scratch_shapes=[
            pltpu.VMEM((D_MODEL, PEER_TOK), jnp.bfloat16),
            gate_scratch, gate_scratch, gate_scratch, gate_scratch,
            pltpu.VMEM((PEER_EB, PEER_TOK), jnp.float32),
            pltpu.VMEM((PEER_EB, PEER_TOK), jnp.bfloat16),
            pltpu.VMEM((D_MODEL, PEER_TOK), jnp.float32),
        ],
        compiler_params=_params(("parallel", "parallel", "arbitrary")),
    )(x, g, wqt, sk, u, vt, fg)


def _prepare_layer(l, w_in, conv_w, swa_sink, w_out, norm_mix_g, norm_xa_g, norm_mem_g, w_xq, w_xk, w_xv, w_xo,
                   norm_ffn_g, peer_wq, peer_subkeys, peer_u, peer_v):
    bf = jnp.bfloat16
    scale = HEAD_DIM ** -0.5
    o = np.cumsum([0, NA_DIM, NA_DIM, NA_DIM, SC_DIM, SC_DIM, SC_DIM, SWA_DIM, SWA_KV_DIM, SWA_KV_DIM])
    wi = w_in[l]
    w_in_l = jnp.concatenate([
        wi[:, o[0]:o[1]] * scale, wi[:, o[1]:o[6]], _swa_regroup(wi[:, o[6]:o[7]] * scale, 1), wi[:, o[7]:o[9]],
    ], axis=1).astype(bf)
    wo = w_out[l]
    row = lambda v: v.reshape(1, -1).astype(jnp.float32)
    return dict(
        g_mix=row(norm_mix_g[l]), w_in=w_in_l, conv_w=conv_w[l].astype(jnp.float32),
        sink=swa_sink[l].astype(jnp.float32),
        wna=wo[:NA_DIM].astype(bf), wsc=wo[NA_DIM:NA_DIM + SC_DIM].astype(bf),
        wsw=_swa_regroup(wo[NA_DIM + SC_DIM:], 0).astype(bf),
        g_xa=row(norm_xa_g[l]), g_mem=row(norm_mem_g[l]),
        wxq=w_xq[l].astype(bf), wxk=w_xk[l].astype(bf), wxv=w_xv[l].astype(bf), wxo=w_xo[l].astype(bf),
        g_ffn=row(norm_ffn_g[l]), wqt=peer_wq[l].T.astype(bf),
        sk=peer_subkeys[l].reshape(2 * PEER_HEADS, PEER_NKEYS, PEER_HALF).astype(bf),
        u=peer_u[l].astype(bf), vt=peer_v[l].T.astype(bf),
    )


def _trunk(x, mem, layers, na_bias, swa_bias, final_g):
    depth = len(layers)
    for l, p in enumerate(layers):
        naq, nak, nav, u, bgate, swq, swk, swv = _mix_in(x, p["g_mix"], p["w_in"])
        yna = _na_attn(naq, nak, nav, na_bias[l])
        ysw = _swa_attn(p["sink"], swq, swk, swv, swa_bias)
        km, vm = _mem_kv(mem, p["g_mem"], p["wxk"], p["wxv"])
        x = _mix_out(x, yna, ysw, u, bgate, p["conv_w"], p["wna"], p["wsc"], p["wsw"],
                     p["g_xa"], p["wxq"], km, vm, p["wxo"])
        x = _peer(x, p["g_ffn"], p["wqt"], p["sk"], p["u"], p["vt"], final_g, final_norm=(l == depth - 1))
    return x


def kernel(x_prompt, x_sample, mem_prompt, mem_sample, norm_mix_g, w_in, na_rpb, conv_w, swa_sink, t5_bias, w_out,
           norm_xa_g, norm_mem_g, w_xq, w_xk, w_xv, w_xo, norm_ffn_g, peer_wq, peer_subkeys, peer_u, peer_v, final_g):
    depth = w_in.shape[0]
    layers = [_prepare_layer(l, w_in, conv_w, swa_sink, w_out, norm_mix_g, norm_xa_g, norm_mem_g, w_xq, w_xk, w_xv,
                             w_xo, norm_ffn_g, peer_wq, peer_subkeys, peer_u, peer_v) for l in range(depth)]
    fg = final_g.reshape(1, -1).astype(jnp.float32)
    tables = {}
    outs = []
    for x, mem in ((x_prompt, mem_prompt), (x_sample, mem_sample)):
        s = x.shape[1]
        if s not in tables:
            tables[s] = ([_na_bias_tables(na_rpb[l], s) for l in range(depth)], _swa_bias_tables(t5_bias, s))
        outs.append(_trunk(x, mem, layers, *tables[s], fg))
    return tuple(outs)
```

```python
import functools

import numpy as np
import jax
import jax.numpy as jnp
from jax import lax
from jax.experimental import pallas as pl
from jax.experimental.pallas import tpu as pltpu

D_MODEL = 1024
GRID_W = 64
HEAD_DIM = 64
NA_HEADS = 6
NA_WIN_H = 8
NA_WIN_W = 16
NA_DIM = NA_HEADS * HEAD_DIM
SC_DIM = 256
SWA_HEADS = 6
SWA_KV_HEADS = 2
SWA_GROUP = SWA_HEADS // SWA_KV_HEADS
SWA_WINDOW = 128
SWA_DIM = SWA_HEADS * HEAD_DIM
SWA_KV_DIM = SWA_KV_HEADS * HEAD_DIM
T5_BUCKETS = 32
T5_MAX_DIST = 128
IN_DIM = 3 * NA_DIM + 3 * SC_DIM + SWA_DIM + 2 * SWA_KV_DIM
XA_HEADS = 4
XA_HEAD_DIM = 128
XA_DIM = XA_HEADS * XA_HEAD_DIM
PEER_HEADS = 8
PEER_NKEYS = 128
PEER_EXPERTS = PEER_NKEYS * PEER_NKEYS
PEER_TOPK = 16
PEER_HALF = 128
RMS_EPS = 1e-6
NEG_INF = -1e30

LANES = 128
SUBLANES = 8
VMEM_LIMIT = 56 * 1024 * 1024

TOK_TILE = 512
NA_Q_ROWS = 4
NA_K_ROWS = NA_Q_ROWS + NA_WIN_H
NA_Q = NA_Q_ROWS * GRID_W
NA_K = NA_K_ROWS * GRID_W
SWA_Q = 256
SWA_K = SWA_Q + 2 * SWA_WINDOW
PEER_TOK = 512
PEER_EB = 2048
PEER_CHUNKS = PEER_TOK // LANES
PEER_Q_PIECES = 4

_NT = (((1,), (1,)), ((), ()))


def _rms(xf, g):
    return xf * lax.rsqrt(jnp.mean(xf * xf, axis=-1, keepdims=True) + RMS_EPS) * g


def _dot(a, b):
    return jnp.dot(a, b, preferred_element_type=jnp.float32)


def _dot_nt(a, b):
    return lax.dot_general(a, b, _NT, preferred_element_type=jnp.float32)


def _params(sem, flags=None):
    return pltpu.CompilerParams(dimension_semantics=sem, vmem_limit_bytes=VMEM_LIMIT, flags=flags)


def _mix_in_kernel(x_ref, g_ref, w_ref, naq_ref, nak_ref, nav_ref, u_ref, b_ref, swq_ref, swk_ref, swv_ref):
    xn = _rms(x_ref[...], g_ref[...]).astype(jnp.bfloat16)
    z = _dot(xn, w_ref[...])
    o = 0
    naq_ref[...] = z[:, o:o + NA_DIM].astype(jnp.bfloat16); o += NA_DIM
    nak_ref[...] = z[:, o:o + NA_DIM].astype(jnp.bfloat16); o += NA_DIM
    nav_ref[...] = z[:, o:o + NA_DIM].astype(jnp.bfloat16); o += NA_DIM
    b_ref[...] = z[:, o:o + SC_DIM]; o += SC_DIM
    u_ref[...] = z[:, o:o + SC_DIM] * z[:, o + SC_DIM:o + 2 * SC_DIM]; o += 2 * SC_DIM
    swq_ref[...] = z[:, o:o + SWA_DIM].astype(jnp.bfloat16); o += SWA_DIM
    swk_ref[...] = z[:, o:o + SWA_KV_DIM].astype(jnp.bfloat16); o += SWA_KV_DIM
    swv_ref[...] = z[:, o:o + SWA_KV_DIM].astype(jnp.bfloat16)


def _mix_in(x, g, w):
    b, s, _ = x.shape
    tok = lambda d: pl.BlockSpec((None, TOK_TILE, d), lambda bi, i: (bi, i, 0))
    full = lambda shape: pl.BlockSpec(shape, lambda bi, i: (0,) * len(shape))
    widths = [(NA_DIM, jnp.bfloat16)] * 3 + [(SC_DIM, jnp.float32)] * 2 + \
             [(SWA_DIM, jnp.bfloat16), (SWA_KV_DIM, jnp.bfloat16), (SWA_KV_DIM, jnp.bfloat16)]
    return pl.pallas_call(
        _mix_in_kernel,
        name="mix_in",
        grid=(b, s // TOK_TILE),
        in_specs=[tok(D_MODEL), full((1, D_MODEL)), full((D_MODEL, IN_DIM))],
        out_specs=[tok(d) for d, _ in widths],
        out_shape=[jax.ShapeDtypeStruct((b, s, d), t) for d, t in widths],
        compiler_params=_params(("parallel", "parallel")),
    )(x, g, w)


def _pair_heads(q_pair, k_pair, v_pair, bias_lo, bias_hi, extra_lo=None, extra_hi=None):
    low = lax.broadcasted_iota(jnp.int32, (1, LANES), 1) < HEAD_DIM
    outs = []
    for keep, bias, extra in ((low, bias_lo, extra_lo), (~low, bias_hi, extra_hi)):
        qm = jnp.where(keep, q_pair, jnp.zeros_like(q_pair))
        logits = _dot_nt(qm, k_pair) + bias
        mx = jnp.max(logits, axis=-1, keepdims=True)
        if extra is not None:
            mx = jnp.maximum(mx, extra)
        p = jnp.exp(logits - mx)
        den = jnp.sum(p, axis=-1, keepdims=True)
        if extra is not None:
            den = den + jnp.exp(extra - mx)
        outs.append(_dot(p.astype(jnp.bfloat16), v_pair) / den)
    return jnp.where(low, outs[0], outs[1])


def _na_kernel(q_ref, k_ref, v_ref, bias_ref, o_ref):
    m = pl.program_id(1)
    rows = k_ref.shape[0] // GRID_W
    start_row = jnp.clip(m * NA_Q_ROWS - NA_WIN_H // 2, 0, rows - NA_K_ROWS)
    start = pl.multiple_of(start_row * GRID_W, GRID_W * NA_Q_ROWS)
    kwin = k_ref[pl.ds(start, NA_K), :]
    vwin = v_ref[pl.ds(start, NA_K), :]
    q = q_ref[...]
    for g in range(NA_HEADS // 2):
        sl = slice(g * LANES, (g + 1) * LANES)
        out = _pair_heads(q[:, sl], kwin[:, sl], vwin[:, sl], bias_ref[2 * g], bias_ref[2 * g + 1])
        o_ref[:, sl] = out.astype(o_ref.dtype)


def _na_case(m, nsteps):
    return jnp.where(m == 0, 0, jnp.where(m == nsteps - 1, 2, 1))


def _na_attn(q, k, v, bias):
    b, s, _ = q.shape
    nsteps = s // NA_Q
    return pl.pallas_call(
        _na_kernel,
        name="na_attn",
        grid=(b, nsteps),
        in_specs=[
            pl.BlockSpec((None, NA_Q, NA_DIM), lambda bi, m: (bi, m, 0)),
            pl.BlockSpec((None, s, NA_DIM), lambda bi, m: (bi, 0, 0)),
            pl.BlockSpec((None, s, NA_DIM), lambda bi, m: (bi, 0, 0)),
            pl.BlockSpec((None, NA_HEADS, NA_Q, NA_K), lambda bi, m: (_na_case(m, nsteps), 0, 0, 0)),
        ],
        out_specs=pl.BlockSpec((None, NA_Q, NA_DIM), lambda bi, m: (bi, m, 0)),
        out_shape=jax.ShapeDtypeStruct((b, s, NA_DIM), jnp.bfloat16),
        compiler_params=_params(("parallel", "arbitrary")),
    )(q, k, v, bias)


def _na_bias_tables(rpb, s):
    rows = s // GRID_W
    nsteps = rows // NA_Q_ROWS

    c = np.arange(GRID_W)[:, None]
    kc = np.arange(GRID_W)[None, :]
    c0 = np.clip(c - NA_WIN_W // 2, 0, GRID_W - NA_WIN_W)
    valid_c = (kc >= c0) & (kc < c0 + NA_WIN_W)
    sel_c = (kc - c + NA_WIN_W - 1)[..., None] == np.arange(2 * NA_WIN_W - 1)

    def row_geometry(m):
        start_row = int(np.clip(m * NA_Q_ROWS - NA_WIN_H // 2, 0, rows - NA_K_ROWS))
        r = (m * NA_Q_ROWS + np.arange(NA_Q_ROWS))[:, None]
        kr = (start_row + np.arange(NA_K_ROWS))[None, :]
        r0 = np.clip(r - NA_WIN_H // 2, 0, rows - NA_WIN_H)
        valid_r = (kr >= r0) & (kr < r0 + NA_WIN_H)
        sel_r = (kr - r + NA_WIN_H - 1)[..., None] == np.arange(2 * NA_WIN_H - 1)
        return valid_r, sel_r

    interior = row_geometry(1)
    for m in range(1, nsteps - 1):
        assert all(np.array_equal(a, b_) for a, b_ in zip(row_geometry(m), interior))
    tabs = []
    for valid_r, sel_r in (row_geometry(0), interior, row_geometry(nsteps - 1)):
        t = jnp.einsum("rka,abh,cjb->hrckj", sel_r.astype(np.float32), rpb.astype(jnp.float32),
                       sel_c.astype(np.float32), precision=lax.Precision.HIGHEST)
        valid = valid_r[:, None, :, None] & valid_c[None, :, None, :]
        tabs.append(jnp.where(valid[None], t, NEG_INF).reshape(NA_HEADS, NA_Q, NA_K))
    return jnp.stack(tabs)


def _swa_kernel(sink_ref, q_ref, k_ref, v_ref, bias_ref, o_ref):
    n = pl.program_id(1)
    s = k_ref.shape[0]
    start = pl.multiple_of(jnp.clip(n * SWA_Q - SWA_WINDOW, 0, s - SWA_K), SWA_WINDOW)
    kwin = k_ref[pl.ds(start, SWA_K), :]
    vwin = v_ref[pl.ds(start, SWA_K), :]
    q = q_ref[...]
    for g in range(SWA_GROUP):
        sl = slice(g * LANES, (g + 1) * LANES)
        out = _pair_heads(q[:, sl], kwin, vwin, bias_ref[2 * g], bias_ref[2 * g + 1],
                          sink_ref[g], sink_ref[g + SWA_GROUP])
        o_ref[:, sl] = out.astype(o_ref.dtype)


def _swa_attn(sink, q, k, v, bias):
    b, s, _ = q.shape
    nsteps = s // SWA_Q
    return pl.pallas_call(
        _swa_kernel,
        name="swa_attn",
        grid=(b, nsteps),
        in_specs=[
            pl.BlockSpec(memory_space=pltpu.SMEM),
            pl.BlockSpec((None, SWA_Q, SWA_DIM), lambda bi, n: (bi, n, 0)),
            pl.BlockSpec((None, s, SWA_KV_DIM), lambda bi, n: (bi, 0, 0)),
            pl.BlockSpec((None, s, SWA_KV_DIM), lambda bi, n: (bi, 0, 0)),
            pl.BlockSpec((None, SWA_HEADS, SWA_Q, SWA_K), lambda bi, n: (_na_case(n, nsteps), 0, 0, 0)),
        ],
        out_specs=pl.BlockSpec((None, SWA_Q, SWA_DIM), lambda bi, n: (bi, n, 0)),
        out_shape=jax.ShapeDtypeStruct((b, s, SWA_DIM), jnp.bfloat16),
        compiler_params=_params(("parallel", "arbitrary")),
    )(sink, q, k, v, bias)


def _t5_bucket(rel):
    nb = T5_BUCKETS // 2
    max_exact = nb // 2
    ret = (rel > 0).astype(np.int32) * nb
    n = np.abs(rel)
    large = max_exact + (np.log(np.maximum(n, 1) / max_exact) / np.log(T5_MAX_DIST / max_exact)
                         * (nb - max_exact)).astype(np.int32)
    large = np.minimum(large, nb - 1)
    return (ret + np.where(n < max_exact, n, large)).astype(np.int32)


_SWA_SLOT_HEADS = [h for g in range(SWA_GROUP) for h in (g, g + SWA_GROUP)]


def _swa_regroup(w, axis):
    blocks = [lax.slice_in_dim(w, h * HEAD_DIM, (h + 1) * HEAD_DIM, axis=axis) for h in _SWA_SLOT_HEADS]
    return jnp.concatenate(blocks, axis=axis)


def _swa_bias_tables(t5_bias, s):
    offsets = np.arange(-SWA_WINDOW, SWA_WINDOW + 1)
    sel = _t5_bucket(offsets)[:, None] == np.arange(T5_BUCKETS)
    bias_off = jnp.dot(sel.astype(np.float32), t5_bias.astype(jnp.float32), precision=lax.Precision.HIGHEST)
    bias_off = jnp.stack([bias_off[:, h] for h in _SWA_SLOT_HEADS])
    nsteps = s // SWA_Q
    period = SWA_Q + SWA_K - 1
    tabs = []
    for n in (0, 1, nsteps - 1):
        start = int(np.clip(n * SWA_Q - SWA_WINDOW, 0, s - SWA_K))
        delta = n * SWA_Q - start
        lead = SWA_Q - 1 + delta - SWA_WINDOW
        g = jnp.concatenate([jnp.full((SWA_HEADS, lead), NEG_INF, jnp.float32), bias_off,
                             jnp.full((SWA_HEADS, period - lead - 2 * SWA_WINDOW - 1), NEG_INF, jnp.float32)], axis=1)
        hankel = jnp.tile(g, (1, SWA_Q + 1))[:, :SWA_Q * (period + 1)].reshape(SWA_HEADS, SWA_Q, period + 1)
        tabs.append(jnp.flip(hankel[:, :, :SWA_K], axis=1))
    return jnp.stack(tabs)


def _mem_kv_kernel(mem_ref, g_ref, wk_ref, wv_ref, k_ref, v_ref):
    mn = _rms(mem_ref[...], g_ref[...]).astype(jnp.bfloat16)
    k_ref[...] = _dot(mn, wk_ref[...]).astype(jnp.bfloat16)
    v_ref[...] = _dot(mn, wv_ref[...]).astype(jnp.bfloat16)


def _mem_kv(mem, g, wk, wv):
    b, m, _ = mem.shape
    full = lambda shape: pl.BlockSpec(shape, lambda bi: (0,) * len(shape))
    return pl.pallas_call(
        _mem_kv_kernel,
        name="mem_kv",
        grid=(b,),
        in_specs=[pl.BlockSpec((None, m, D_MODEL), lambda bi: (bi, 0, 0)), full((1, D_MODEL)),
                  full((D_MODEL, XA_DIM)), full((D_MODEL, XA_DIM))],
        out_specs=[pl.BlockSpec((None, m, XA_DIM), lambda bi: (bi, 0, 0))] * 2,
        out_shape=[jax.ShapeDtypeStruct((b, m, XA_DIM), jnp.bfloat16)] * 2,
        compiler_params=_params(("parallel",)),
    )(mem, g, wk, wv)


def _mix_out_kernel(x_ref, yna_ref, ysw_ref, u_ref, uprev_ref, unext_ref, b_ref, cw_ref,
                    wna_ref, wsc_ref, wsw_ref, g_ref, wq_ref, km_ref, vm_ref, wo_ref, o_ref):
    i = pl.program_id(1)
    nt = pl.num_programs(1)
    u = u_ref[...]
    t = u.shape[0]
    row = lax.broadcasted_iota(jnp.int32, (t, 1), 0)
    prev_row = jnp.where(i > 0, uprev_ref[SUBLANES - 1:SUBLANES, :], 0.0)
    next_row = jnp.where(i < nt - 1, unext_ref[0:1, :], 0.0)
    u_m1 = jnp.where(row == 0, prev_row, pltpu.roll(u, 1, axis=0))
    u_p1 = jnp.where(row == t - 1, next_row, pltpu.roll(u, t - 1, axis=0))
    ysc = b_ref[...] * (u_m1 * cw_ref[0:1, :] + u * cw_ref[1:2, :] + u_p1 * cw_ref[2:3, :])
    y = _dot(yna_ref[...], wna_ref[...]) + _dot(ysc.astype(jnp.bfloat16), wsc_ref[...]) \
        + _dot(ysw_ref[...], wsw_ref[...])
    x1 = x_ref[...] + y

    xn = _rms(x1, g_ref[...]).astype(jnp.bfloat16)
    q = _dot(xn, wq_ref[...])
    outs = []
    for h in range(XA_HEADS):
        sl = slice(h * XA_HEAD_DIM, (h + 1) * XA_HEAD_DIM)
        logits = _dot_nt(q[:, sl].astype(jnp.bfloat16), km_ref[:, sl]) * (XA_HEAD_DIM ** -0.5)
        mx = jnp.max(logits, axis=-1, keepdims=True)
        p = jnp.exp(logits - mx)
        den = jnp.sum(p, axis=-1, keepdims=True)
        outs.append((_dot(p.astype(jnp.bfloat16), vm_ref[:, sl]) / den).astype(jnp.bfloat16))
    o = jnp.concatenate(outs, axis=-1)
    o_ref[...] = x1 + _dot(o, wo_ref[...])


def _mix_out(x, yna, ysw, u, bgate, cw, wna, wsc, wsw, g, wq, km, vm, wo):
    b, s, _ = x.shape
    nt = s // TOK_TILE
    halo_blocks = TOK_TILE // SUBLANES
    last_halo = s // SUBLANES - 1
    tok = lambda d: pl.BlockSpec((None, TOK_TILE, d), lambda bi, i: (bi, i, 0))
    full = lambda shape: pl.BlockSpec(shape, lambda bi, i: (0,) * len(shape))
    mem = pl.BlockSpec((None, km.shape[1], XA_DIM), lambda bi, i: (bi, 0, 0))
    return pl.pallas_call(
        _mix_out_kernel,
        name="mix_out",
        grid=(b, nt),
        in_specs=[
            tok(D_MODEL), tok(NA_DIM), tok(SWA_DIM), tok(SC_DIM),
            pl.BlockSpec((None, SUBLANES, SC_DIM), lambda bi, i: (bi, jnp.maximum(i * halo_blocks - 1, 0), 0)),
            pl.BlockSpec((None, SUBLANES, SC_DIM), lambda bi, i: (bi, jnp.minimum((i + 1) * halo_blocks, last_halo), 0)),
            tok(SC_DIM), full((3, SC_DIM)),
            full((NA_DIM, D_MODEL)), full((SC_DIM, D_MODEL)), full((SWA_DIM, D_MODEL)),
            full((1, D_MODEL)), full((D_MODEL, XA_DIM)), mem, mem, full((XA_DIM, D_MODEL)),
        ],
        out_specs=tok(D_MODEL),
        out_shape=jax.ShapeDtypeStruct((b, s, D_MODEL), jnp.float32),
        compiler_params=_params(("parallel", "parallel")),
    )(x, yna, ysw, u, u, u, bgate, cw, wna, wsc, wsw, g, wq, km, vm, wo)


_CAND_PAIRS = [(p, q) for p in range(PEER_TOPK) for q in range(PEER_TOPK) if (p + 1) * (q + 1) <= PEER_TOPK]
_CAND_VREGS = -(-len(_CAND_PAIRS) // SUBLANES)


def _allmax_sublanes(v):
    for shift in (4, 2, 1):
        v = jnp.maximum(v, pltpu.roll(v, shift, axis=0))
    return v


_NO_RANK = 255.0


def _top_values(s3, count):
    vals = []
    for r in range(count):
        m = _allmax_sublanes(jnp.max(s3, axis=0))
        vals.append(m)
        if r + 1 < count:
            s3 = jnp.where(s3 == m[None], -jnp.inf, s3)
    return vals


def _batcher_pairs(n):
    pairs = []
    p = 1
    while p < n:
        k = p
        while k >= 1:
            for j in range(k % p, n - k, 2 * k):
                for i in range(min(k, n - j - k)):
                    if (i + j) // (2 * p) == (i + j + k) // (2 * p):
                        pairs.append((i + j, i + j + k))
            k //= 2
        p *= 2
    return pairs


_SORT16 = _batcher_pairs(PEER_TOPK)
_BITONIC16 = [(i, i + k) for k in (8, 4, 2, 1) for i in range(PEER_TOPK) if not i & k]


def _descending(vals, pairs):
    vals = list(vals)
    for i, j in pairs:
        vals[i], vals[j] = jnp.maximum(vals[i], vals[j]), jnp.minimum(vals[i], vals[j])
    return vals


def _top16_sorted(s3):
    vals = _descending([s3[k] for k in range(PEER_TOPK)], _SORT16)
    for shift in (4, 2, 1):
        other = [pltpu.roll(v, shift, axis=0) for v in vals]
        vals = _descending([jnp.maximum(vals[k], other[PEER_TOPK - 1 - k]) for k in range(PEER_TOPK)], _BITONIC16)
    return vals


def _pack_bf16_pair(x):
    bits = pltpu.bitcast(x.astype(jnp.bfloat16).astype(jnp.float32), jnp.uint32)
    return bits | (bits >> 16)


def _peer_prep_chunk(s0, s1):
    groups = PEER_NKEYS // SUBLANES
    s0g = s0.reshape(groups, SUBLANES, LANES)
    s1g = s1.reshape(groups, SUBLANES, LANES)
    a = _top16_sorted(s0g)
    b = _top16_sorted(s1g)
    sub = lax.broadcasted_iota(jnp.int32, (SUBLANES, LANES), 0)
    cand = []
    for v in range(_CAND_VREGS):
        acc = jnp.full((SUBLANES, LANES), -jnp.inf, jnp.float32)
        for k, (p, q) in enumerate(_CAND_PAIRS[v * SUBLANES:(v + 1) * SUBLANES]):
            acc = jnp.where(sub == k, a[p] + b[q], acc)
        cand.append(acc)
    best = _top_values(jnp.stack(cand), PEER_TOPK)
    z = jnp.ones((SUBLANES, LANES), jnp.float32)
    for r in range(1, PEER_TOPK):
        z = z + jnp.exp(best[r] - best[0])
    tau = best[PEER_TOPK - 1]
    count = jnp.zeros(s0g.shape, jnp.float32)
    rank = jnp.full(s1g.shape, _NO_RANK, jnp.float32)
    for p in range(PEER_TOPK):
        n_p = jnp.zeros((SUBLANES, LANES), jnp.float32)
        for q in range(PEER_TOPK // (p + 1)):
            n_p = n_p + jnp.where(a[p] + b[q] >= tau, 1.0, 0.0)
        count = jnp.where(s0g == a[p][None], n_p[None], count)
        rank = jnp.where(s1g == b[p][None], float(p), rank)
    rowgate = jnp.exp(s0 - a[0][0:1, :]) / z[0:1, :]
    colgate = jnp.exp(s1 - b[0][0:1, :])
    return (_pack_bf16_pair(count.reshape(PEER_NKEYS, LANES)), _pack_bf16_pair(rowgate),
            rank.reshape(PEER_NKEYS, LANES).astype(jnp.bfloat16), colgate.astype(jnp.bfloat16))


def _bf16_row(ref, h, c, i):
    word = ref[h, c, pl.ds(i, 1), :]
    return pltpu.bitcast(jnp.broadcast_to(word, (PEER_NKEYS // 2, LANES)), jnp.bfloat16)


def _peer_gate_rows(eb, ii, count_ref, rowg_ref, rank_ref, colg_ref, tile_ref, gt_ref):
    i_glob = eb * (PEER_EB // PEER_NKEYS) + ii
    rows = pl.ds(pl.multiple_of(ii * PEER_NKEYS, PEER_NKEYS), PEER_NKEYS)
    for c in range(PEER_CHUNKS):
        sl = slice(c * LANES, (c + 1) * LANES)
        w = jnp.zeros((PEER_NKEYS, LANES), jnp.bfloat16)
        for h in range(PEER_HEADS):
            selected = rank_ref[h, c] < _bf16_row(count_ref, h, c, i_glob)
            w = w + jnp.where(selected, colg_ref[h, c], jnp.zeros((), jnp.bfloat16)) * _bf16_row(rowg_ref, h, c, i_glob)
        hh = tile_ref[ii, c]
        act = 0.5 * hh * (1.0 + lax.erf(hh * (2.0 ** -0.5)))
        gt_ref[rows, sl] = w * act.astype(jnp.bfloat16)


def _peer_kernel(x_ref, g_ref, wqt_ref, sk_ref, u_ref, vt_ref, fg_ref, o_ref,
                 xnt_ref, qt_ref, tile_ref, count_ref, rowg_ref, rank_ref, colg_ref, gt_ref, acc_ref,
                 *, final_norm):
    eb = pl.program_id(2)
    neb = pl.num_programs(2)

    @pl.when(eb == 0)
    def _prep():
        xn = _rms(x_ref[...], g_ref[...])
        xnt_ref[...] = xn.T.astype(jnp.bfloat16)

        q_rows = 2 * PEER_HEADS * PEER_HALF // PEER_Q_PIECES
        for k in range(PEER_Q_PIECES):
            rows = slice(k * q_rows, (k + 1) * q_rows)
            qt_ref[rows, :] = _dot(wqt_ref[rows, :], xnt_ref[...]).astype(jnp.bfloat16)

        def scores(hp, carry):
            s = _dot(sk_ref[hp], qt_ref[pl.ds(pl.multiple_of(hp * PEER_HALF, PEER_HALF), PEER_HALF), :])
            for c in range(PEER_CHUNKS):
                tile_ref[hp, c] = s[:, c * LANES:(c + 1) * LANES]
            return carry

        lax.fori_loop(0, 2 * PEER_HEADS, scores, 0)

        def chunk_pair(idx, carry):
            h = idx // (PEER_CHUNKS // 2)
            for c in (2 * (idx % (PEER_CHUNKS // 2)), 2 * (idx % (PEER_CHUNKS // 2)) + 1):
                count_ref[h, c], rowg_ref[h, c], rank_ref[h, c], colg_ref[h, c] = _peer_prep_chunk(
                    tile_ref[2 * h, c], tile_ref[2 * h + 1, c])
            return carry

        lax.fori_loop(0, PEER_HEADS * PEER_CHUNKS // 2, chunk_pair, 0)

    hidden = _dot(u_ref[...], xnt_ref[...])
    for ii in range(PEER_EB // PEER_NKEYS):
        for c in range(PEER_CHUNKS):
            tile_ref[ii, c] = hidden[ii * PEER_NKEYS:(ii + 1) * PEER_NKEYS, c * LANES:(c + 1) * LANES]

    def row_block(ii, carry):
        _peer_gate_rows(eb, ii, count_ref, rowg_ref, rank_ref, colg_ref, tile_ref, gt_ref)
        return carry

    lax.fori_loop(0, PEER_EB // PEER_NKEYS, row_block, 0)

    contrib = _dot(vt_ref[...], gt_ref[...])

    @pl.when(eb == 0)
    def _init():
        acc_ref[...] = contrib

    @pl.when(eb > 0)
    def _accum():
        acc_ref[...] += contrib

    @pl.when(eb == neb - 1)
    def _finish():
        y = x_ref[...] + acc_ref[...].T
        if final_norm:
            y = _rms(y, fg_ref[...])
        o_ref[...] = y


def _peer(x, g, wqt, sk, u, vt, fg, final_norm):
    b, s, _ = x.shape
    tok = pl.BlockSpec((None, PEER_TOK, D_MODEL), lambda bi, i, e: (bi, i, 0))
    full = lambda shape: pl.BlockSpec(shape, lambda bi, i, e: (0,) * len(shape), pipeline_mode=pl.Buffered(1))
    row_scratch = pltpu.VMEM((PEER_HEADS, PEER_CHUNKS, PEER_NKEYS, LANES), jnp.uint32)
    col_scratch = pltpu.VMEM((PEER_HEADS, PEER_CHUNKS, PEER_NKEYS, LANES), jnp.bfloat16)
    return pl.pallas_call(
        functools.partial(_peer_kernel, final_norm=final_norm),
        name="peer",
        grid=(b, s // PEER_TOK, PEER_EXPERTS // PEER_EB),
        in_specs=[
            tok, full((1, D_MODEL)), full((2 * PEER_HEADS * PEER_HALF, D_MODEL)),
            full((2 * PEER_HEADS, PEER_NKEYS, PEER_HALF)),
            pl.BlockSpec((PEER_EB, D_MODEL), lambda bi, i, e: (e, 0)),
            pl.BlockSpec((D_MODEL, PEER_EB), lambda bi, i, e: (0, e)),
            full((1, D_MODEL)),
        ],
        out_specs=tok,
        out_shape=jax.ShapeDtypeStruct((b, s, D_MODEL), jnp.float32),
        scratch_shapes=[
            pltpu.VMEM((D_MODEL, PEER_TOK), jnp.bfloat16),
            pltpu.VMEM((2 * PEER_HEADS * PEER_HALF, PEER_TOK), jnp.bfloat16),
            pltpu.VMEM((max(2 * PEER_HEADS, PEER_EB // PEER_NKEYS), PEER_CHUNKS, PEER_NKEYS, LANES), jnp.float32),
            row_scratch, row_scratch, col_scratch, col_scratch,
            pltpu.VMEM((PEER_EB, PEER_TOK), jnp.bfloat16),
            pltpu.VMEM((D_MODEL, PEER_TOK), jnp.float32),
        ],
        compiler_params=_params(("parallel", "parallel", "arbitrary")),
    )(x, g, wqt, sk, u, vt, fg)


def _prepare_layer(l, w_in, conv_w, swa_sink, w_out, norm_mix_g, norm_xa_g, norm_mem_g, w_xq, w_xk, w_xv, w_xo,
                   norm_ffn_g, peer_wq, peer_subkeys, peer_u, peer_v):
    bf = jnp.bfloat16
    scale = HEAD_DIM ** -0.5
    o = np.cumsum([0, NA_DIM, NA_DIM, NA_DIM, SC_DIM, SC_DIM, SC_DIM, SWA_DIM, SWA_KV_DIM, SWA_KV_DIM])
    wi = w_in[l]
    w_in_l = jnp.concatenate([
        wi[:, o[0]:o[1]] * scale, wi[:, o[1]:o[6]], _swa_regroup(wi[:, o[6]:o[7]] * scale, 1), wi[:, o[7]:o[9]],
    ], axis=1).astype(bf)
    wo = w_out[l]
    row = lambda v: v.reshape(1, -1).astype(jnp.float32)
    return dict(
        g_mix=row(norm_mix_g[l]), w_in=w_in_l, conv_w=conv_w[l].astype(jnp.float32),
        sink=swa_sink[l].astype(jnp.float32),
        wna=wo[:NA_DIM].astype(bf), wsc=wo[NA_DIM:NA_DIM + SC_DIM].astype(bf),
        wsw=_swa_regroup(wo[NA_DIM + SC_DIM:], 0).astype(bf),
        g_xa=row(norm_xa_g[l]), g_mem=row(norm_mem_g[l]),
        wxq=w_xq[l].astype(bf), wxk=w_xk[l].astype(bf), wxv=w_xv[l].astype(bf), wxo=w_xo[l].astype(bf),
        g_ffn=row(norm_ffn_g[l]), wqt=peer_wq[l].T.astype(bf),
        sk=peer_subkeys[l].reshape(2 * PEER_HEADS, PEER_NKEYS, PEER_HALF).astype(bf),
        u=peer_u[l].astype(bf), vt=peer_v[l].T.astype(bf),
    )


def _trunk(x, mem, layers, na_bias, swa_bias, final_g):
    depth = len(layers)
    for l, p in enumerate(layers):
        naq, nak, nav, u, bgate, swq, swk, swv = _mix_in(x, p["g_mix"], p["w_in"])
        yna = _na_attn(naq, nak, nav, na_bias[l])
        ysw = _swa_attn(p["sink"], swq, swk, swv, swa_bias)
        km, vm = _mem_kv(mem, p["g_mem"], p["wxk"], p["wxv"])
        x = _mix_out(x, yna, ysw, u, bgate, p["conv_w"], p["wna"], p["wsc"], p["wsw"],
                     p["g_xa"], p["wxq"], km, vm, p["wxo"])
        x = _peer(x, p["g_ffn"], p["wqt"], p["sk"], p["u"], p["vt"], final_g, final_norm=(l == depth - 1))
    return x


def kernel(x_prompt, x_sample, mem_prompt, mem_sample, norm_mix_g, w_in, na_rpb, conv_w, swa_sink, t5_bias, w_out,
           norm_xa_g, norm_mem_g, w_xq, w_xk, w_xv, w_xo, norm_ffn_g, peer_wq, peer_subkeys, peer_u, peer_v, final_g):
    depth = w_in.shape[0]
    layers = [_prepare_layer(l, w_in, conv_w, swa_sink, w_out, norm_mix_g, norm_xa_g, norm_mem_g, w_xq, w_xk, w_xv,
                             w_xo, norm_ffn_g, peer_wq, peer_subkeys, peer_u, peer_v) for l in range(depth)]
    fg = final_g.reshape(1, -1).astype(jnp.float32)
    tables = {}
    outs = []
    for x, mem in ((x_prompt, mem_prompt), (x_sample, mem_sample)):
        s = x.shape[1]
        if s not in tables:
            tables[s] = ([_na_bias_tables(na_rpb[l], s) for l in range(depth)], _swa_bias_tables(t5_bias, s))
        outs.append(_trunk(x, mem, layers, *tables[s], fg))
    return tuple(outs)
```

```python
import functools

import numpy as np
import jax
import jax.numpy as jnp
from jax import lax
from jax.experimental import pallas as pl
from jax.experimental.pallas import tpu as pltpu

D_MODEL = 1024
GRID_W = 64
HEAD_DIM = 64
NA_HEADS = 6
NA_WIN_H = 8
NA_WIN_W = 16
NA_DIM = NA_HEADS * HEAD_DIM
SC_DIM = 256
SWA_HEADS = 6
SWA_KV_HEADS = 2
SWA_GROUP = SWA_HEADS // SWA_KV_HEADS
SWA_WINDOW = 128
SWA_DIM = SWA_HEADS * HEAD_DIM
SWA_KV_DIM = SWA_KV_HEADS * HEAD_DIM
T5_BUCKETS = 32
T5_MAX_DIST = 128
IN_DIM = 3 * NA_DIM + 3 * SC_DIM + SWA_DIM + 2 * SWA_KV_DIM
XA_HEADS = 4
XA_HEAD_DIM = 128
XA_DIM = XA_HEADS * XA_HEAD_DIM
PEER_HEADS = 8
PEER_NKEYS = 128
PEER_EXPERTS = PEER_NKEYS * PEER_NKEYS
PEER_TOPK = 16
PEER_HALF = 128
RMS_EPS = 1e-6
NEG_INF = -1e30

LANES = 128
SUBLANES = 8
VMEM_LIMIT = 56 * 1024 * 1024

TOK_TILE = 512
NA_Q_ROWS = 4
NA_K_ROWS = NA_Q_ROWS + NA_WIN_H
NA_Q = NA_Q_ROWS * GRID_W
NA_K = NA_K_ROWS * GRID_W
SWA_Q = 256
SWA_K = SWA_Q + 2 * SWA_WINDOW
PEER_TOK = 512
PEER_EB = 2048
PEER_CHUNKS = PEER_TOK // LANES
PEER_Q_PIECES = 4

_NT = (((1,), (1,)), ((), ()))


def _rms(xf, g):
    return xf * lax.rsqrt(jnp.mean(xf * xf, axis=-1, keepdims=True) + RMS_EPS) * g


def _dot(a, b):
    return jnp.dot(a, b, preferred_element_type=jnp.float32)


def _dot_nt(a, b):
    return lax.dot_general(a, b, _NT, preferred_element_type=jnp.float32)


def _params(sem, flags=None):
    return pltpu.CompilerParams(dimension_semantics=sem, vmem_limit_bytes=VMEM_LIMIT, flags=flags)


def _mix_in_kernel(x_ref, g_ref, w_ref, naq_ref, nak_ref, nav_ref, u_ref, b_ref, swq_ref, swk_ref, swv_ref):
    xn = _rms(x_ref[...], g_ref[...]).astype(jnp.bfloat16)
    z = _dot(xn, w_ref[...])
    o = 0
    naq_ref[...] = z[:, o:o + NA_DIM].astype(jnp.bfloat16); o += NA_DIM
    nak_ref[...] = z[:, o:o + NA_DIM].astype(jnp.bfloat16); o += NA_DIM
    nav_ref[...] = z[:, o:o + NA_DIM].astype(jnp.bfloat16); o += NA_DIM
    b_ref[...] = z[:, o:o + SC_DIM]; o += SC_DIM
    u_ref[...] = z[:, o:o + SC_DIM] * z[:, o + SC_DIM:o + 2 * SC_DIM]; o += 2 * SC_DIM
    swq_ref[...] = z[:, o:o + SWA_DIM].astype(jnp.bfloat16); o += SWA_DIM
    swk_ref[...] = z[:, o:o + SWA_KV_DIM].astype(jnp.bfloat16); o += SWA_KV_DIM
    swv_ref[...] = z[:, o:o + SWA_KV_DIM].astype(jnp.bfloat16)


def _mix_in(x, g, w):
    b, s, _ = x.shape
    tok = lambda d: pl.BlockSpec((None, TOK_TILE, d), lambda bi, i: (bi, i, 0))
    full = lambda shape: pl.BlockSpec(shape, lambda bi, i: (0,) * len(shape))
    widths = [(NA_DIM, jnp.bfloat16)] * 3 + [(SC_DIM, jnp.float32)] * 2 + \
             [(SWA_DIM, jnp.bfloat16), (SWA_KV_DIM, jnp.bfloat16), (SWA_KV_DIM, jnp.bfloat16)]
    return pl.pallas_call(
        _mix_in_kernel,
        name="mix_in",
        grid=(b, s // TOK_TILE),
        in_specs=[tok(D_MODEL), full((1, D_MODEL)), full((D_MODEL, IN_DIM))],
        out_specs=[tok(d) for d, _ in widths],
        out_shape=[jax.ShapeDtypeStruct((b, s, d), t) for d, t in widths],
        compiler_params=_params(("parallel", "parallel")),
    )(x, g, w)


def _pair_heads(q_pair, k_pair, v_pair, bias_lo, bias_hi, extra_lo=None, extra_hi=None):
    low = lax.broadcasted_iota(jnp.int32, (1, LANES), 1) < HEAD_DIM
    outs = []
    for keep, bias, extra in ((low, bias_lo, extra_lo), (~low, bias_hi, extra_hi)):
        qm = jnp.where(keep, q_pair, jnp.zeros_like(q_pair))
        logits = _dot_nt(qm, k_pair) + bias
        mx = jnp.max(logits, axis=-1, keepdims=True)
        if extra is not None:
            mx = jnp.maximum(mx, extra)
        p = jnp.exp(logits - mx)
        den = jnp.sum(p, axis=-1, keepdims=True)
        if extra is not None:
            den = den + jnp.exp(extra - mx)
        outs.append(_dot(p.astype(jnp.bfloat16), v_pair) / den)
    return jnp.where(low, outs[0], outs[1])


def _na_kernel(q_ref, k_ref, v_ref, bias_ref, o_ref):
    m = pl.program_id(1)
    rows = k_ref.shape[0] // GRID_W
    start_row = jnp.clip(m * NA_Q_ROWS - NA_WIN_H // 2, 0, rows - NA_K_ROWS)
    start = pl.multiple_of(start_row * GRID_W, GRID_W * NA_Q_ROWS)
    kwin = k_ref[pl.ds(start, NA_K), :]
    vwin = v_ref[pl.ds(start, NA_K), :]
    q = q_ref[...]
    for g in range(NA_HEADS // 2):
        sl = slice(g * LANES, (g + 1) * LANES)
        out = _pair_heads(q[:, sl], kwin[:, sl], vwin[:, sl], bias_ref[2 * g], bias_ref[2 * g + 1])
        o_ref[:, sl] = out.astype(o_ref.dtype)


def _na_case(m, nsteps):
    return jnp.where(m == 0, 0, jnp.where(m == nsteps - 1, 2, 1))


def _na_attn(q, k, v, bias):
    b, s, _ = q.shape
    nsteps = s // NA_Q
    return pl.pallas_call(
        _na_kernel,
        name="na_attn",
        grid=(b, nsteps),
        in_specs=[
            pl.BlockSpec((None, NA_Q, NA_DIM), lambda bi, m: (bi, m, 0)),
            pl.BlockSpec((None, s, NA_DIM), lambda bi, m: (bi, 0, 0)),
            pl.BlockSpec((None, s, NA_DIM), lambda bi, m: (bi, 0, 0)),
            pl.BlockSpec((None, NA_HEADS, NA_Q, NA_K), lambda bi, m: (_na_case(m, nsteps), 0, 0, 0)),
        ],
        out_specs=pl.BlockSpec((None, NA_Q, NA_DIM), lambda bi, m: (bi, m, 0)),
        out_shape=jax.ShapeDtypeStruct((b, s, NA_DIM), jnp.bfloat16),
        compiler_params=_params(("parallel", "arbitrary")),
    )(q, k, v, bias)


def _na_bias_tables(rpb, s):
    rows = s // GRID_W
    nsteps = rows // NA_Q_ROWS

    c = np.arange(GRID_W)[:, None]
    kc = np.arange(GRID_W)[None, :]
    c0 = np.clip(c - NA_WIN_W // 2, 0, GRID_W - NA_WIN_W)
    valid_c = (kc >= c0) & (kc < c0 + NA_WIN_W)
    sel_c = (kc - c + NA_WIN_W - 1)[..., None] == np.arange(2 * NA_WIN_W - 1)

    def row_geometry(m):
        start_row = int(np.clip(m * NA_Q_ROWS - NA_WIN_H // 2, 0, rows - NA_K_ROWS))
        r = (m * NA_Q_ROWS + np.arange(NA_Q_ROWS))[:, None]
        kr = (start_row + np.arange(NA_K_ROWS))[None, :]
        r0 = np.clip(r - NA_WIN_H // 2, 0, rows - NA_WIN_H)
        valid_r = (kr >= r0) & (kr < r0 + NA_WIN_H)
        sel_r = (kr - r + NA_WIN_H - 1)[..., None] == np.arange(2 * NA_WIN_H - 1)
        return valid_r, sel_r

    interior = row_geometry(1)
    for m in range(1, nsteps - 1):
        assert all(np.array_equal(a, b_) for a, b_ in zip(row_geometry(m), interior))
    tabs = []
    for valid_r, sel_r in (row_geometry(0), interior, row_geometry(nsteps - 1)):
        t = jnp.einsum("rka,abh,cjb->hrckj", sel_r.astype(np.float32), rpb.astype(jnp.float32),
                       sel_c.astype(np.float32), precision=lax.Precision.HIGHEST)
        valid = valid_r[:, None, :, None] & valid_c[None, :, None, :]
        tabs.append(jnp.where(valid[None], t, NEG_INF).reshape(NA_HEADS, NA_Q, NA_K))
    return jnp.stack(tabs)


def _swa_kernel(sink_ref, q_ref, k_ref, v_ref, bias_ref, o_ref):
    n = pl.program_id(1)
    s = k_ref.shape[0]
    start = pl.multiple_of(jnp.clip(n * SWA_Q - SWA_WINDOW, 0, s - SWA_K), SWA_WINDOW)
    kwin = k_ref[pl.ds(start, SWA_K), :]
    vwin = v_ref[pl.ds(start, SWA_K), :]
    q = q_ref[...]
    for g in range(SWA_GROUP):
        sl = slice(g * LANES, (g + 1) * LANES)
        out = _pair_heads(q[:, sl], kwin, vwin, bias_ref[2 * g], bias_ref[2 * g + 1],
                          sink_ref[g], sink_ref[g + SWA_GROUP])
        o_ref[:, sl] = out.astype(o_ref.dtype)


def _swa_attn(sink, q, k, v, bias):
    b, s, _ = q.shape
    nsteps = s // SWA_Q
    return pl.pallas_call(
        _swa_kernel,
        name="swa_attn",
        grid=(b, nsteps),
        in_specs=[
            pl.BlockSpec(memory_space=pltpu.SMEM),
            pl.BlockSpec((None, SWA_Q, SWA_DIM), lambda bi, n: (bi, n, 0)),
            pl.BlockSpec((None, s, SWA_KV_DIM), lambda bi, n: (bi, 0, 0)),
            pl.BlockSpec((None, s, SWA_KV_DIM), lambda bi, n: (bi, 0, 0)),
            pl.BlockSpec((None, SWA_HEADS, SWA_Q, SWA_K), lambda bi, n: (_na_case(n, nsteps), 0, 0, 0)),
        ],
        out_specs=pl.BlockSpec((None, SWA_Q, SWA_DIM), lambda bi, n: (bi, n, 0)),
        out_shape=jax.ShapeDtypeStruct((b, s, SWA_DIM), jnp.bfloat16),
        compiler_params=_params(("parallel", "arbitrary")),
    )(sink, q, k, v, bias)


def _t5_bucket(rel):
    nb = T5_BUCKETS // 2
    max_exact = nb // 2
    ret = (rel > 0).astype(np.int32) * nb
    n = np.abs(rel)
    large = max_exact + (np.log(np.maximum(n, 1) / max_exact) / np.log(T5_MAX_DIST / max_exact)
                         * (nb - max_exact)).astype(np.int32)
    large = np.minimum(large, nb - 1)
    return (ret + np.where(n < max_exact, n, large)).astype(np.int32)


_SWA_SLOT_HEADS = [h for g in range(SWA_GROUP) for h in (g, g + SWA_GROUP)]


def _swa_regroup(w, axis):
    blocks = [lax.slice_in_dim(w, h * HEAD_DIM, (h + 1) * HEAD_DIM, axis=axis) for h in _SWA_SLOT_HEADS]
    return jnp.concatenate(blocks, axis=axis)


def _swa_bias_tables(t5_bias, s):
    offsets = np.arange(-SWA_WINDOW, SWA_WINDOW + 1)
    sel = _t5_bucket(offsets)[:, None] == np.arange(T5_BUCKETS)
    bias_off = jnp.dot(sel.astype(np.float32), t5_bias.astype(jnp.float32), precision=lax.Precision.HIGHEST)
    bias_off = jnp.stack([bias_off[:, h] for h in _SWA_SLOT_HEADS])
    nsteps = s // SWA_Q
    period = SWA_Q + SWA_K - 1
    tabs = []
    for n in (0, 1, nsteps - 1):
        start = int(np.clip(n * SWA_Q - SWA_WINDOW, 0, s - SWA_K))
        delta = n * SWA_Q - start
        lead = SWA_Q - 1 + delta - SWA_WINDOW
        g = jnp.concatenate([jnp.full((SWA_HEADS, lead), NEG_INF, jnp.float32), bias_off,
                             jnp.full((SWA_HEADS, period - lead - 2 * SWA_WINDOW - 1), NEG_INF, jnp.float32)], axis=1)
        hankel = jnp.tile(g, (1, SWA_Q + 1))[:, :SWA_Q * (period + 1)].reshape(SWA_HEADS, SWA_Q, period + 1)
        tabs.append(jnp.flip(hankel[:, :, :SWA_K], axis=1))
    return jnp.stack(tabs)


def _mem_kv_kernel(mem_ref, g_ref, wk_ref, wv_ref, k_ref, v_ref):
    mn = _rms(mem_ref[...], g_ref[...]).astype(jnp.bfloat16)
    k_ref[...] = _dot(mn, wk_ref[...]).astype(jnp.bfloat16)
    v_ref[...] = _dot(mn, wv_ref[...]).astype(jnp.bfloat16)


def _mem_kv(mem, g, wk, wv):
    b, m, _ = mem.shape
    full = lambda shape: pl.BlockSpec(shape, lambda bi: (0,) * len(shape))
    return pl.pallas_call(
        _mem_kv_kernel,
        name="mem_kv",
        grid=(b,),
        in_specs=[pl.BlockSpec((None, m, D_MODEL), lambda bi: (bi, 0, 0)), full((1, D_MODEL)),
                  full((D_MODEL, XA_DIM)), full((D_MODEL, XA_DIM))],
        out_specs=[pl.BlockSpec((None, m, XA_DIM), lambda bi: (bi, 0, 0))] * 2,
        out_shape=[jax.ShapeDtypeStruct((b, m, XA_DIM), jnp.bfloat16)] * 2,
        compiler_params=_params(("parallel",)),
    )(mem, g, wk, wv)


def _mix_out_kernel(x_ref, yna_ref, ysw_ref, u_ref, uprev_ref, unext_ref, b_ref, cw_ref,
                    wna_ref, wsc_ref, wsw_ref, g_ref, wq_ref, km_ref, vm_ref, wo_ref, o_ref):
    i = pl.program_id(1)
    nt = pl.num_programs(1)
    u = u_ref[...]
    t = u.shape[0]
    row = lax.broadcasted_iota(jnp.int32, (t, 1), 0)
    prev_row = jnp.where(i > 0, uprev_ref[SUBLANES - 1:SUBLANES, :], 0.0)
    next_row = jnp.where(i < nt - 1, unext_ref[0:1, :], 0.0)
    u_m1 = jnp.where(row == 0, prev_row, pltpu.roll(u, 1, axis=0))
    u_p1 = jnp.where(row == t - 1, next_row, pltpu.roll(u, t - 1, axis=0))
    ysc = b_ref[...] * (u_m1 * cw_ref[0:1, :] + u * cw_ref[1:2, :] + u_p1 * cw_ref[2:3, :])
    y = _dot(yna_ref[...], wna_ref[...]) + _dot(ysc.astype(jnp.bfloat16), wsc_ref[...]) \
        + _dot(ysw_ref[...], wsw_ref[...])
    x1 = x_ref[...] + y

    xn = _rms(x1, g_ref[...]).astype(jnp.bfloat16)
    q = _dot(xn, wq_ref[...])
    outs = []
    for h in range(XA_HEADS):
        sl = slice(h * XA_HEAD_DIM, (h + 1) * XA_HEAD_DIM)
        logits = _dot_nt(q[:, sl].astype(jnp.bfloat16), km_ref[:, sl]) * (XA_HEAD_DIM ** -0.5)
        mx = jnp.max(logits, axis=-1, keepdims=True)
        p = jnp.exp(logits - mx)
        den = jnp.sum(p, axis=-1, keepdims=True)
        outs.append((_dot(p.astype(jnp.bfloat16), vm_ref[:, sl]) / den).astype(jnp.bfloat16))
    o = jnp.concatenate(outs, axis=-1)
    o_ref[...] = x1 + _dot(o, wo_ref[...])


def _mix_out(x, yna, ysw, u, bgate, cw, wna, wsc, wsw, g, wq, km, vm, wo):
    b, s, _ = x.shape
    nt = s // TOK_TILE
    halo_blocks = TOK_TILE // SUBLANES
    last_halo = s // SUBLANES - 1
    tok = lambda d: pl.BlockSpec((None, TOK_TILE, d), lambda bi, i: (bi, i, 0))
    full = lambda shape: pl.BlockSpec(shape, lambda bi, i: (0,) * len(shape))
    mem = pl.BlockSpec((None, km.shape[1], XA_DIM), lambda bi, i: (bi, 0, 0))
    return pl.pallas_call(
        _mix_out_kernel,
        name="mix_out",
        grid=(b, nt),
        in_specs=[
            tok(D_MODEL), tok(NA_DIM), tok(SWA_DIM), tok(SC_DIM),
            pl.BlockSpec((None, SUBLANES, SC_DIM), lambda bi, i: (bi, jnp.maximum(i * halo_blocks - 1, 0), 0)),
            pl.BlockSpec((None, SUBLANES, SC_DIM), lambda bi, i: (bi, jnp.minimum((i + 1) * halo_blocks, last_halo), 0)),
            tok(SC_DIM), full((3, SC_DIM)),
            full((NA_DIM, D_MODEL)), full((SC_DIM, D_MODEL)), full((SWA_DIM, D_MODEL)),
            full((1, D_MODEL)), full((D_MODEL, XA_DIM)), mem, mem, full((XA_DIM, D_MODEL)),
        ],
        out_specs=tok(D_MODEL),
        out_shape=jax.ShapeDtypeStruct((b, s, D_MODEL), jnp.float32),
        compiler_params=_params(("parallel", "parallel")),
    )(x, yna, ysw, u, u, u, bgate, cw, wna, wsc, wsw, g, wq, km, vm, wo)


_CAND_PAIRS = [(p, q) for p in range(PEER_TOPK) for q in range(PEER_TOPK) if (p + 1) * (q + 1) <= PEER_TOPK]
_CAND_VREGS = -(-len(_CAND_PAIRS) // SUBLANES)


def _allmax_sublanes(v):
    for shift in (4, 2, 1):
        v = jnp.maximum(v, pltpu.roll(v, shift, axis=0))
    return v


def _top_values(s3, count):
    vals = []
    for r in range(count):
        m = _allmax_sublanes(jnp.max(s3, axis=0))
        vals.append(m)
        if r + 1 < count:
            s3 = jnp.where(s3 == m[None], -jnp.inf, s3)
    return vals


def _batcher_pairs(n):
    pairs = []
    p = 1
    while p < n:
        k = p
        while k >= 1:
            for j in range(k % p, n - k, 2 * k):
                for i in range(min(k, n - j - k)):
                    if (i + j) // (2 * p) == (i + j + k) // (2 * p):
                        pairs.append((i + j, i + j + k))
            k //= 2
        p *= 2
    return pairs


_SORT16 = _batcher_pairs(PEER_TOPK)
_BITONIC16 = [(i, i + k) for k in (8, 4, 2, 1) for i in range(PEER_TOPK) if not i & k]


def _descending(vals, pairs):
    vals = list(vals)
    for i, j in pairs:
        vals[i], vals[j] = jnp.maximum(vals[i], vals[j]), jnp.minimum(vals[i], vals[j])
    return vals


def _top16_sorted(s3):
    vals = _descending([s3[k] for k in range(PEER_TOPK)], _SORT16)
    for shift in (4, 2, 1):
        other = [pltpu.roll(v, shift, axis=0) for v in vals]
        vals = _descending([jnp.maximum(vals[k], other[PEER_TOPK - 1 - k]) for k in range(PEER_TOPK)], _BITONIC16)
    return vals


def _peer_prep_chunk(s0, s1):
    groups = PEER_NKEYS // SUBLANES
    a = _top16_sorted(s0.reshape(groups, SUBLANES, LANES))
    b = _top16_sorted(s1.reshape(groups, SUBLANES, LANES))
    sub = lax.broadcasted_iota(jnp.int32, (SUBLANES, LANES), 0)
    cand = []
    for v in range(_CAND_VREGS):
        acc = jnp.full((SUBLANES, LANES), -jnp.inf, jnp.float32)
        for k, (p, q) in enumerate(_CAND_PAIRS[v * SUBLANES:(v + 1) * SUBLANES]):
            acc = jnp.where(sub == k, a[p] + b[q], acc)
        cand.append(acc)
    best = _top_values(jnp.stack(cand), PEER_TOPK)
    z = jnp.ones((SUBLANES, LANES), jnp.float32)
    for r in range(1, PEER_TOPK):
        z = z + jnp.exp(best[r] - best[0])
    tau = best[PEER_TOPK - 1][0:1, :]
    floor = jnp.exp((tau - b[0][0:1, :]) - s0)
    rowgate = jnp.exp(s0 - a[0][0:1, :]) / z[0:1, :]
    colgate = jnp.exp(s1 - b[0][0:1, :])
    return floor, rowgate, colgate


def _row(words, k):
    return jnp.broadcast_to(words[k:k + 1, :], (PEER_NKEYS, LANES))


def _peer_gate_rows(eb, grp, floor_ref, rowg_ref, colg_ref, tile_ref, gt_ref):
    first = grp * SUBLANES
    i0 = pl.multiple_of(eb * (PEER_EB // PEER_NKEYS) + first, SUBLANES)
    for c in range(PEER_CHUNKS):
        sl = slice(c * LANES, (c + 1) * LANES)
        for k in range(SUBLANES):
            w = jnp.zeros((PEER_NKEYS, LANES), jnp.float32)
            for h in range(PEER_HEADS):
                colgate = colg_ref[h, c]
                selected = colgate >= _row(floor_ref[h, c, pl.ds(i0, SUBLANES), :], k)
                w = w + jnp.where(selected, colgate, 0.0) * _row(rowg_ref[h, c, pl.ds(i0, SUBLANES), :], k)
            hh = tile_ref[first + k, c]
            act = 0.5 * hh * (1.0 + lax.erf(hh * (2.0 ** -0.5)))
            rows = pl.ds(pl.multiple_of((first + k) * PEER_NKEYS, PEER_NKEYS), PEER_NKEYS)
            gt_ref[rows, sl] = (w * act).astype(jnp.bfloat16)


def _peer_kernel(x_ref, g_ref, wqt_ref, sk_ref, u_ref, vt_ref, fg_ref, o_ref,
                 xnt_ref, qt_ref, tile_ref, floor_ref, rowg_ref, colg_ref, gt_ref, acc_ref, *, final_norm):
    eb = pl.program_id(2)
    neb = pl.num_programs(2)

    @pl.when(eb == 0)
    def _prep():
        xn = _rms(x_ref[...], g_ref[...])
        xnt_ref[...] = xn.T.astype(jnp.bfloat16)

        q_rows = 2 * PEER_HEADS * PEER_HALF // PEER_Q_PIECES
        for k in range(PEER_Q_PIECES):
            rows = slice(k * q_rows, (k + 1) * q_rows)
            qt_ref[rows, :] = _dot(wqt_ref[rows, :], xnt_ref[...]).astype(jnp.bfloat16)

        def scores(hp, carry):
            s = _dot(sk_ref[hp], qt_ref[pl.ds(pl.multiple_of(hp * PEER_HALF, PEER_HALF), PEER_HALF), :])
            for c in range(PEER_CHUNKS):
                tile_ref[hp, c] = s[:, c * LANES:(c + 1) * LANES]
            return carry

        lax.fori_loop(0, 2 * PEER_HEADS, scores, 0)

        def chunk_pair(idx, carry):
            h = idx // (PEER_CHUNKS // 2)
            for c in (2 * (idx % (PEER_CHUNKS // 2)), 2 * (idx % (PEER_CHUNKS // 2)) + 1):
                floor_ref[h, c], rowg_ref[h, c], colg_ref[h, c] = _peer_prep_chunk(
                    tile_ref[2 * h, c], tile_ref[2 * h + 1, c])
            return carry

        lax.fori_loop(0, PEER_HEADS * PEER_CHUNKS // 2, chunk_pair, 0)

    hidden = _dot(u_ref[...], xnt_ref[...])
    for ii in range(PEER_EB // PEER_NKEYS):
        for c in range(PEER_CHUNKS):
            tile_ref[ii, c] = hidden[ii * PEER_NKEYS:(ii + 1) * PEER_NKEYS, c * LANES:(c + 1) * LANES]

    def row_group(grp, carry):
        _peer_gate_rows(eb, grp, floor_ref, rowg_ref, colg_ref, tile_ref, gt_ref)
        return carry

    lax.fori_loop(0, PEER_EB // PEER_NKEYS // SUBLANES, row_group, 0)

    contrib = _dot(vt_ref[...], gt_ref[...])

    @pl.when(eb == 0)
    def _init():
        acc_ref[...] = contrib

    @pl.when(eb > 0)
    def _accum():
        acc_ref[...] += contrib

    @pl.when(eb == neb - 1)
    def _finish():
        y = x_ref[...] + acc_ref[...].T
        if final_norm:
            y = _rms(y, fg_ref[...])
        o_ref[...] = y


def _peer(x, g, wqt, sk, u, vt, fg, final_norm):
    b, s, _ = x.shape
    tok = pl.BlockSpec((None, PEER_TOK, D_MODEL), lambda bi, i, e: (bi, i, 0))
    full = lambda shape: pl.BlockSpec(shape, lambda bi, i, e: (0,) * len(shape), pipeline_mode=pl.Buffered(1))
    gate_scratch = pltpu.VMEM((PEER_HEADS, PEER_CHUNKS, PEER_NKEYS, LANES), jnp.float32)
    return pl.pallas_call(
        functools.partial(_peer_kernel, final_norm=final_norm),
        name="peer",
        grid=(b, s // PEER_TOK, PEER_EXPERTS // PEER_EB),
        in_specs=[
            tok, full((1, D_MODEL)), full((2 * PEER_HEADS * PEER_HALF, D_MODEL)),
            full((2 * PEER_HEADS, PEER_NKEYS, PEER_HALF)),
            pl.BlockSpec((PEER_EB, D_MODEL), lambda bi, i, e: (e, 0)),
            pl.BlockSpec((D_MODEL, PEER_EB), lambda bi, i, e: (0, e)),
            full((1, D_MODEL)),
        ],
        out_specs=tok,
        out_shape=jax.ShapeDtypeStruct((b, s, D_MODEL), jnp.float32),
        scratch_shapes=[
            pltpu.VMEM((D_MODEL, PEER_TOK), jnp.bfloat16),
            pltpu.VMEM((2 * PEER_HEADS * PEER_HALF, PEER_TOK), jnp.bfloat16),
            pltpu.VMEM((max(2 * PEER_HEADS, PEER_EB // PEER_NKEYS), PEER_CHUNKS, PEER_NKEYS, LANES), jnp.float32),
            gate_scratch, gate_scratch, gate_scratch,
            pltpu.VMEM((PEER_EB, PEER_TOK), jnp.bfloat16),
            pltpu.VMEM((D_MODEL, PEER_TOK), jnp.float32),
        ],
        compiler_params=_params(("parallel", "parallel", "arbitrary")),
    )(x, g, wqt, sk, u, vt, fg)


def _prepare_layer(l, w_in, conv_w, swa_sink, w_out, norm_mix_g, norm_xa_g, norm_mem_g, w_xq, w_xk, w_xv, w_xo,
                   norm_ffn_g, peer_wq, peer_subkeys, peer_u, peer_v):
    bf = jnp.bfloat16
    scale = HEAD_DIM ** -0.5
    o = np.cumsum([0, NA_DIM, NA_DIM, NA_DIM, SC_DIM, SC_DIM, SC_DIM, SWA_DIM, SWA_KV_DIM, SWA_KV_DIM])
    wi = w_in[l]
    w_in_l = jnp.concatenate([
        wi[:, o[0]:o[1]] * scale, wi[:, o[1]:o[6]], _swa_regroup(wi[:, o[6]:o[7]] * scale, 1), wi[:, o[7]:o[9]],
    ], axis=1).astype(bf)
    wo = w_out[l]
    row = lambda v: v.reshape(1, -1).astype(jnp.float32)
    return dict(
        g_mix=row(norm_mix_g[l]), w_in=w_in_l, conv_w=conv_w[l].astype(jnp.float32),
        sink=swa_sink[l].astype(jnp.float32),
        wna=wo[:NA_DIM].astype(bf), wsc=wo[NA_DIM:NA_DIM + SC_DIM].astype(bf),
        wsw=_swa_regroup(wo[NA_DIM + SC_DIM:], 0).astype(bf),
        g_xa=row(norm_xa_g[l]), g_mem=row(norm_mem_g[l]),
        wxq=w_xq[l].astype(bf), wxk=w_xk[l].astype(bf), wxv=w_xv[l].astype(bf), wxo=w_xo[l].astype(bf),
        g_ffn=row(norm_ffn_g[l]), wqt=peer_wq[l].T.astype(bf),
        sk=peer_subkeys[l].reshape(2 * PEER_HEADS, PEER_NKEYS, PEER_HALF).astype(bf),
        u=peer_u[l].astype(bf), vt=peer_v[l].T.astype(bf),
    )


def _trunk(x, mem, layers, na_bias, swa_bias, final_g):
    depth = len(layers)
    for l, p in enumerate(layers):
        naq, nak, nav, u, bgate, swq, swk, swv = _mix_in(x, p["g_mix"], p["w_in"])
        yna = _na_attn(naq, nak, nav, na_bias[l])
        ysw = _swa_attn(p["sink"], swq, swk, swv, swa_bias)
        km, vm = _mem_kv(mem, p["g_mem"], p["wxk"], p["wxv"])
        x = _mix_out(x, yna, ysw, u, bgate, p["conv_w"], p["wna"], p["wsc"], p["wsw"],
                     p["g_xa"], p["wxq"], km, vm, p["wxo"])
        x = _peer(x, p["g_ffn"], p["wqt"], p["sk"], p["u"], p["vt"], final_g, final_norm=(l == depth - 1))
    return x


def kernel(x_prompt, x_sample, mem_prompt, mem_sample, norm_mix_g, w_in, na_rpb, conv_w, swa_sink, t5_bias, w_out,
           norm_xa_g, norm_mem_g, w_xq, w_xk, w_xv, w_xo, norm_ffn_g, peer_wq, peer_subkeys, peer_u, peer_v, final_g):
    depth = w_in.shape[0]
    layers = [_prepare_layer(l, w_in, conv_w, swa_sink, w_out, norm_mix_g, norm_xa_g, norm_mem_g, w_xq, w_xk, w_xv,
                             w_xo, norm_ffn_g, peer_wq, peer_subkeys, peer_u, peer_v) for l in range(depth)]
    fg = final_g.reshape(1, -1).astype(jnp.float32)
    tables = {}
    outs = []
    for x, mem in ((x_prompt, mem_prompt), (x_sample, mem_sample)):
        s = x.shape[1]
        if s not in tables:
            tables[s] = ([_na_bias_tables(na_rpb[l], s) for l in range(depth)], _swa_bias_tables(t5_bias, s))
        outs.append(_trunk(x, mem, layers, *tables[s], fg))
    return tuple(outs)
```

```python
import functools

import numpy as np
import jax
import jax.numpy as jnp
from jax import lax
from jax.experimental import pallas as pl
from jax.experimental.pallas import tpu as pltpu

D_MODEL = 1024
GRID_W = 64
HEAD_DIM = 64
NA_HEADS = 6
NA_WIN_H = 8
NA_WIN_W = 16
NA_DIM = NA_HEADS * HEAD_DIM
SC_DIM = 256
SWA_HEADS = 6
SWA_KV_HEADS = 2
SWA_GROUP = SWA_HEADS // SWA_KV_HEADS
SWA_WINDOW = 128
SWA_DIM = SWA_HEADS * HEAD_DIM
SWA_KV_DIM = SWA_KV_HEADS * HEAD_DIM
T5_BUCKETS = 32
T5_MAX_DIST = 128
IN_DIM = 3 * NA_DIM + 3 * SC_DIM + SWA_DIM + 2 * SWA_KV_DIM
XA_HEADS = 4
XA_HEAD_DIM = 128
XA_DIM = XA_HEADS * XA_HEAD_DIM
PEER_HEADS = 8
PEER_NKEYS = 128
PEER_EXPERTS = PEER_NKEYS * PEER_NKEYS
PEER_TOPK = 16
PEER_HALF = 128
RMS_EPS = 1e-6
NEG_INF = -1e30

LANES = 128
SUBLANES = 8
VMEM_LIMIT = 56 * 1024 * 1024

TOK_TILE = 512
NA_Q_ROWS = 4
NA_K_ROWS = NA_Q_ROWS + NA_WIN_H
NA_Q = NA_Q_ROWS * GRID_W
NA_K = NA_K_ROWS * GRID_W
SWA_Q = 256
SWA_K = SWA_Q + 2 * SWA_WINDOW
PEER_TOK = 512
PEER_EB = 2048
PEER_CHUNKS = PEER_TOK // LANES
PEER_V_SLAB = 256
PEER_Q_PIECES = 4

_NT = (((1,), (1,)), ((), ()))


def _rms(xf, g):
    return xf * lax.rsqrt(jnp.mean(xf * xf, axis=-1, keepdims=True) + RMS_EPS) * g


def _dot(a, b):
    return jnp.dot(a, b, preferred_element_type=jnp.float32)


def _dot_nt(a, b):
    return lax.dot_general(a, b, _NT, preferred_element_type=jnp.float32)


def _params(sem, flags=None):
    return pltpu.CompilerParams(dimension_semantics=sem, vmem_limit_bytes=VMEM_LIMIT, flags=flags)


def _mix_in_kernel(x_ref, g_ref, w_ref, naq_ref, nak_ref, nav_ref, u_ref, b_ref, swq_ref, swk_ref, swv_ref):
    xn = _rms(x_ref[...], g_ref[...]).astype(jnp.bfloat16)
    z = _dot(xn, w_ref[...])
    o = 0
    naq_ref[...] = z[:, o:o + NA_DIM].astype(jnp.bfloat16); o += NA_DIM
    nak_ref[...] = z[:, o:o + NA_DIM].astype(jnp.bfloat16); o += NA_DIM
    nav_ref[...] = z[:, o:o + NA_DIM].astype(jnp.bfloat16); o += NA_DIM
    b_ref[...] = z[:, o:o + SC_DIM]; o += SC_DIM
    u_ref[...] = z[:, o:o + SC_DIM] * z[:, o + SC_DIM:o + 2 * SC_DIM]; o += 2 * SC_DIM
    swq_ref[...] = z[:, o:o + SWA_DIM].astype(jnp.bfloat16); o += SWA_DIM
    swk_ref[...] = z[:, o:o + SWA_KV_DIM].astype(jnp.bfloat16); o += SWA_KV_DIM
    swv_ref[...] = z[:, o:o + SWA_KV_DIM].astype(jnp.bfloat16)


def _mix_in(x, g, w):
    b, s, _ = x.shape
    tok = lambda d: pl.BlockSpec((None, TOK_TILE, d), lambda bi, i: (bi, i, 0))
    full = lambda shape: pl.BlockSpec(shape, lambda bi, i: (0,) * len(shape))
    widths = [(NA_DIM, jnp.bfloat16)] * 3 + [(SC_DIM, jnp.float32)] * 2 + \
             [(SWA_DIM, jnp.bfloat16), (SWA_KV_DIM, jnp.bfloat16), (SWA_KV_DIM, jnp.bfloat16)]
    return pl.pallas_call(
        _mix_in_kernel,
        name="mix_in",
        grid=(b, s // TOK_TILE),
        in_specs=[tok(D_MODEL), full((1, D_MODEL)), full((D_MODEL, IN_DIM))],
        out_specs=[tok(d) for d, _ in widths],
        out_shape=[jax.ShapeDtypeStruct((b, s, d), t) for d, t in widths],
        compiler_params=_params(("parallel", "parallel")),
    )(x, g, w)


def _pair_heads(q_pair, k_pair, v_pair, bias_lo, bias_hi, extra_lo=None, extra_hi=None):
    low = lax.broadcasted_iota(jnp.int32, (1, LANES), 1) < HEAD_DIM
    outs = []
    for keep, bias, extra in ((low, bias_lo, extra_lo), (~low, bias_hi, extra_hi)):
        qm = jnp.where(keep, q_pair, jnp.zeros_like(q_pair))
        logits = _dot_nt(qm, k_pair) + bias
        mx = jnp.max(logits, axis=-1, keepdims=True)
        if extra is not None:
            mx = jnp.maximum(mx, extra)
        p = jnp.exp(logits - mx)
        den = jnp.sum(p, axis=-1, keepdims=True)
        if extra is not None:
            den = den + jnp.exp(extra - mx)
        outs.append(_dot(p.astype(jnp.bfloat16), v_pair) / den)
    return jnp.where(low, outs[0], outs[1])


def _na_kernel(q_ref, k_ref, v_ref, bias_ref, o_ref):
    m = pl.program_id(1)
    rows = k_ref.shape[0] // GRID_W
    start_row = jnp.clip(m * NA_Q_ROWS - NA_WIN_H // 2, 0, rows - NA_K_ROWS)
    start = pl.multiple_of(start_row * GRID_W, GRID_W * NA_Q_ROWS)
    kwin = k_ref[pl.ds(start, NA_K), :]
    vwin = v_ref[pl.ds(start, NA_K), :]
    q = q_ref[...]
    for g in range(NA_HEADS // 2):
        sl = slice(g * LANES, (g + 1) * LANES)
        out = _pair_heads(q[:, sl], kwin[:, sl], vwin[:, sl], bias_ref[2 * g], bias_ref[2 * g + 1])
        o_ref[:, sl] = out.astype(o_ref.dtype)


def _na_case(m, nsteps):
    return jnp.where(m == 0, 0, jnp.where(m == nsteps - 1, 2, 1))


def _na_attn(q, k, v, bias):
    b, s, _ = q.shape
    nsteps = s // NA_Q
    return pl.pallas_call(
        _na_kernel,
        name="na_attn",
        grid=(b, nsteps),
        in_specs=[
            pl.BlockSpec((None, NA_Q, NA_DIM), lambda bi, m: (bi, m, 0)),
            pl.BlockSpec((None, s, NA_DIM), lambda bi, m: (bi, 0, 0)),
            pl.BlockSpec((None, s, NA_DIM), lambda bi, m: (bi, 0, 0)),
            pl.BlockSpec((None, NA_HEADS, NA_Q, NA_K), lambda bi, m: (_na_case(m, nsteps), 0, 0, 0)),
        ],
        out_specs=pl.BlockSpec((None, NA_Q, NA_DIM), lambda bi, m: (bi, m, 0)),
        out_shape=jax.ShapeDtypeStruct((b, s, NA_DIM), jnp.bfloat16),
        compiler_params=_params(("parallel", "arbitrary")),
    )(q, k, v, bias)


def _na_bias_tables(rpb, s):
    rows = s // GRID_W
    nsteps = rows // NA_Q_ROWS

    c = np.arange(GRID_W)[:, None]
    kc = np.arange(GRID_W)[None, :]
    c0 = np.clip(c - NA_WIN_W // 2, 0, GRID_W - NA_WIN_W)
    valid_c = (kc >= c0) & (kc < c0 + NA_WIN_W)
    sel_c = (kc - c + NA_WIN_W - 1)[..., None] == np.arange(2 * NA_WIN_W - 1)

    def row_geometry(m):
        start_row = int(np.clip(m * NA_Q_ROWS - NA_WIN_H // 2, 0, rows - NA_K_ROWS))
        r = (m * NA_Q_ROWS + np.arange(NA_Q_ROWS))[:, None]
        kr = (start_row + np.arange(NA_K_ROWS))[None, :]
        r0 = np.clip(r - NA_WIN_H // 2, 0, rows - NA_WIN_H)
        valid_r = (kr >= r0) & (kr < r0 + NA_WIN_H)
        sel_r = (kr - r + NA_WIN_H - 1)[..., None] == np.arange(2 * NA_WIN_H - 1)
        return valid_r, sel_r

    interior = row_geometry(1)
    for m in range(1, nsteps - 1):
        assert all(np.array_equal(a, b_) for a, b_ in zip(row_geometry(m), interior))
    tabs = []
    for valid_r, sel_r in (row_geometry(0), interior, row_geometry(nsteps - 1)):
        t = jnp.einsum("rka,abh,cjb->hrckj", sel_r.astype(np.float32), rpb.astype(jnp.float32),
                       sel_c.astype(np.float32), precision=lax.Precision.HIGHEST)
        valid = valid_r[:, None, :, None] & valid_c[None, :, None, :]
        tabs.append(jnp.where(valid[None], t, NEG_INF).reshape(NA_HEADS, NA_Q, NA_K))
    return jnp.stack(tabs)


def _swa_kernel(sink_ref, q_ref, k_ref, v_ref, bias_ref, o_ref):
    n = pl.program_id(1)
    s = k_ref.shape[0]
    start = pl.multiple_of(jnp.clip(n * SWA_Q - SWA_WINDOW, 0, s - SWA_K), SWA_WINDOW)
    kwin = k_ref[pl.ds(start, SWA_K), :]
    vwin = v_ref[pl.ds(start, SWA_K), :]
    q = q_ref[...]
    for g in range(SWA_GROUP):
        sl = slice(g * LANES, (g + 1) * LANES)
        out = _pair_heads(q[:, sl], kwin, vwin, bias_ref[2 * g], bias_ref[2 * g + 1],
                          sink_ref[g], sink_ref[g + SWA_GROUP])
        o_ref[:, sl] = out.astype(o_ref.dtype)


def _swa_attn(sink, q, k, v, bias):
    b, s, _ = q.shape
    nsteps = s // SWA_Q
    return pl.pallas_call(
        _swa_kernel,
        name="swa_attn",
        grid=(b, nsteps),
        in_specs=[
            pl.BlockSpec(memory_space=pltpu.SMEM),
            pl.BlockSpec((None, SWA_Q, SWA_DIM), lambda bi, n: (bi, n, 0)),
            pl.BlockSpec((None, s, SWA_KV_DIM), lambda bi, n: (bi, 0, 0)),
            pl.BlockSpec((None, s, SWA_KV_DIM), lambda bi, n: (bi, 0, 0)),
            pl.BlockSpec((None, SWA_HEADS, SWA_Q, SWA_K), lambda bi, n: (_na_case(n, nsteps), 0, 0, 0)),
        ],
        out_specs=pl.BlockSpec((None, SWA_Q, SWA_DIM), lambda bi, n: (bi, n, 0)),
        out_shape=jax.ShapeDtypeStruct((b, s, SWA_DIM), jnp.bfloat16),
        compiler_params=_params(("parallel", "arbitrary")),
    )(sink, q, k, v, bias)


def _t5_bucket(rel):
    nb = T5_BUCKETS // 2
    max_exact = nb // 2
    ret = (rel > 0).astype(np.int32) * nb
    n = np.abs(rel)
    large = max_exact + (np.log(np.maximum(n, 1) / max_exact) / np.log(T5_MAX_DIST / max_exact)
                         * (nb - max_exact)).astype(np.int32)
    large = np.minimum(large, nb - 1)
    return (ret + np.where(n < max_exact, n, large)).astype(np.int32)


_SWA_SLOT_HEADS = [h for g in range(SWA_GROUP) for h in (g, g + SWA_GROUP)]


def _swa_regroup(w, axis):
    blocks = [lax.slice_in_dim(w, h * HEAD_DIM, (h + 1) * HEAD_DIM, axis=axis) for h in _SWA_SLOT_HEADS]
    return jnp.concatenate(blocks, axis=axis)


def _swa_bias_tables(t5_bias, s):
    offsets = np.arange(-SWA_WINDOW, SWA_WINDOW + 1)
    sel = _t5_bucket(offsets)[:, None] == np.arange(T5_BUCKETS)
    bias_off = jnp.dot(sel.astype(np.float32), t5_bias.astype(jnp.float32), precision=lax.Precision.HIGHEST)
    bias_off = jnp.stack([bias_off[:, h] for h in _SWA_SLOT_HEADS])
    nsteps = s // SWA_Q
    period = SWA_Q + SWA_K - 1
    tabs = []
    for n in (0, 1, nsteps - 1):
        start = int(np.clip(n * SWA_Q - SWA_WINDOW, 0, s - SWA_K))
        delta = n * SWA_Q - start
        lead = SWA_Q - 1 + delta - SWA_WINDOW
        g = jnp.concatenate([jnp.full((SWA_HEADS, lead), NEG_INF, jnp.float32), bias_off,
                             jnp.full((SWA_HEADS, period - lead - 2 * SWA_WINDOW - 1), NEG_INF, jnp.float32)], axis=1)
        hankel = jnp.tile(g, (1, SWA_Q + 1))[:, :SWA_Q * (period + 1)].reshape(SWA_HEADS, SWA_Q, period + 1)
        tabs.append(jnp.flip(hankel[:, :, :SWA_K], axis=1))
    return jnp.stack(tabs)


def _mem_kv_kernel(mem_ref, g_ref, wk_ref, wv_ref, k_ref, v_ref):
    mn = _rms(mem_ref[...], g_ref[...]).astype(jnp.bfloat16)
    k_ref[...] = _dot(mn, wk_ref[...]).astype(jnp.bfloat16)
    v_ref[...] = _dot(mn, wv_ref[...]).astype(jnp.bfloat16)


def _mem_kv(mem, g, wk, wv):
    b, m, _ = mem.shape
    full = lambda shape: pl.BlockSpec(shape, lambda bi: (0,) * len(shape))
    return pl.pallas_call(
        _mem_kv_kernel,
        name="mem_kv",
        grid=(b,),
        in_specs=[pl.BlockSpec((None, m, D_MODEL), lambda bi: (bi, 0, 0)), full((1, D_MODEL)),
                  full((D_MODEL, XA_DIM)), full((D_MODEL, XA_DIM))],
        out_specs=[pl.BlockSpec((None, m, XA_DIM), lambda bi: (bi, 0, 0))] * 2,
        out_shape=[jax.ShapeDtypeStruct((b, m, XA_DIM), jnp.bfloat16)] * 2,
        compiler_params=_params(("parallel",)),
    )(mem, g, wk, wv)


def _mix_out_kernel(x_ref, yna_ref, ysw_ref, u_ref, uprev_ref, unext_ref, b_ref, cw_ref,
                    wna_ref, wsc_ref, wsw_ref, g_ref, wq_ref, km_ref, vm_ref, wo_ref, o_ref):
    i = pl.program_id(1)
    nt = pl.num_programs(1)
    u = u_ref[...]
    t = u.shape[0]
    row = lax.broadcasted_iota(jnp.int32, (t, 1), 0)
    prev_row = jnp.where(i > 0, uprev_ref[SUBLANES - 1:SUBLANES, :], 0.0)
    next_row = jnp.where(i < nt - 1, unext_ref[0:1, :], 0.0)
    u_m1 = jnp.where(row == 0, prev_row, pltpu.roll(u, 1, axis=0))
    u_p1 = jnp.where(row == t - 1, next_row, pltpu.roll(u, t - 1, axis=0))
    ysc = b_ref[...] * (u_m1 * cw_ref[0:1, :] + u * cw_ref[1:2, :] + u_p1 * cw_ref[2:3, :])
    y = _dot(yna_ref[...], wna_ref[...]) + _dot(ysc.astype(jnp.bfloat16), wsc_ref[...]) \
        + _dot(ysw_ref[...], wsw_ref[...])
    x1 = x_ref[...] + y

    xn = _rms(x1, g_ref[...]).astype(jnp.bfloat16)
    q = _dot(xn, wq_ref[...])
    outs = []
    for h in range(XA_HEADS):
        sl = slice(h * XA_HEAD_DIM, (h + 1) * XA_HEAD_DIM)
        logits = _dot_nt(q[:, sl].astype(jnp.bfloat16), km_ref[:, sl]) * (XA_HEAD_DIM ** -0.5)
        mx = jnp.max(logits, axis=-1, keepdims=True)
        p = jnp.exp(logits - mx)
        den = jnp.sum(p, axis=-1, keepdims=True)
        outs.append((_dot(p.astype(jnp.bfloat16), vm_ref[:, sl]) / den).astype(jnp.bfloat16))
    o = jnp.concatenate(outs, axis=-1)
    o_ref[...] = x1 + _dot(o, wo_ref[...])


def _mix_out(x, yna, ysw, u, bgate, cw, wna, wsc, wsw, g, wq, km, vm, wo):
    b, s, _ = x.shape
    nt = s // TOK_TILE
    halo_blocks = TOK_TILE // SUBLANES
    last_halo = s // SUBLANES - 1
    tok = lambda d: pl.BlockSpec((None, TOK_TILE, d), lambda bi, i: (bi, i, 0))
    full = lambda shape: pl.BlockSpec(shape, lambda bi, i: (0,) * len(shape))
    mem = pl.BlockSpec((None, km.shape[1], XA_DIM), lambda bi, i: (bi, 0, 0))
    return pl.pallas_call(
        _mix_out_kernel,
        name="mix_out",
        grid=(b, nt),
        in_specs=[
            tok(D_MODEL), tok(NA_DIM), tok(SWA_DIM), tok(SC_DIM),
            pl.BlockSpec((None, SUBLANES, SC_DIM), lambda bi, i: (bi, jnp.maximum(i * halo_blocks - 1, 0), 0)),
            pl.BlockSpec((None, SUBLANES, SC_DIM), lambda bi, i: (bi, jnp.minimum((i + 1) * halo_blocks, last_halo), 0)),
            tok(SC_DIM), full((3, SC_DIM)),
            full((NA_DIM, D_MODEL)), full((SC_DIM, D_MODEL)), full((SWA_DIM, D_MODEL)),
            full((1, D_MODEL)), full((D_MODEL, XA_DIM)), mem, mem, full((XA_DIM, D_MODEL)),
        ],
        out_specs=tok(D_MODEL),
        out_shape=jax.ShapeDtypeStruct((b, s, D_MODEL), jnp.float32),
        compiler_params=_params(("parallel", "parallel")),
    )(x, yna, ysw, u, u, u, bgate, cw, wna, wsc, wsw, g, wq, km, vm, wo)


_CAND_PAIRS = [(p, q) for p in range(PEER_TOPK) for q in range(PEER_TOPK) if (p + 1) * (q + 1) <= PEER_TOPK]
_CAND_VREGS = -(-len(_CAND_PAIRS) // SUBLANES)


def _allmax_sublanes(v):
    for shift in (4, 2, 1):
        v = jnp.maximum(v, pltpu.roll(v, shift, axis=0))
    return v


def _top_values(s3, count):
    vals = []
    for r in range(count):
        m = _allmax_sublanes(jnp.max(s3, axis=0))
        vals.append(m)
        if r + 1 < count:
            s3 = jnp.where(s3 == m[None], -jnp.inf, s3)
    return vals


def _batcher_pairs(n):
    pairs = []
    p = 1
    while p < n:
        k = p
        while k >= 1:
            for j in range(k % p, n - k, 2 * k):
                for i in range(min(k, n - j - k)):
                    if (i + j) // (2 * p) == (i + j + k) // (2 * p):
                        pairs.append((i + j, i + j + k))
            k //= 2
        p *= 2
    return pairs


_SORT16 = _batcher_pairs(PEER_TOPK)
_BITONIC16 = [(i, i + k) for k in (8, 4, 2, 1) for i in range(PEER_TOPK) if not i & k]


def _descending(vals, pairs):
    vals = list(vals)
    for i, j in pairs:
        vals[i], vals[j] = jnp.maximum(vals[i], vals[j]), jnp.minimum(vals[i], vals[j])
    return vals


def _top16_sorted(s3):
    vals = _descending([s3[k] for k in range(PEER_TOPK)], _SORT16)
    for shift in (4, 2, 1):
        other = [pltpu.roll(v, shift, axis=0) for v in vals]
        vals = _descending([jnp.maximum(vals[k], other[PEER_TOPK - 1 - k]) for k in range(PEER_TOPK)], _BITONIC16)
    return vals


def _peer_prep_chunk(s0, s1):
    groups = PEER_NKEYS // SUBLANES
    a = _top16_sorted(s0.reshape(groups, SUBLANES, LANES))
    b = _top16_sorted(s1.reshape(groups, SUBLANES, LANES))
    sub = lax.broadcasted_iota(jnp.int32, (SUBLANES, LANES), 0)
    cand = []
    for v in range(_CAND_VREGS):
        acc = jnp.full((SUBLANES, LANES), -jnp.inf, jnp.float32)
        for k, (p, q) in enumerate(_CAND_PAIRS[v * SUBLANES:(v + 1) * SUBLANES]):
            acc = jnp.where(sub == k, a[p] + b[q], acc)
        cand.append(acc)
    best = _top_values(jnp.stack(cand), PEER_TOPK)
    z = jnp.ones((SUBLANES, LANES), jnp.float32)
    for r in range(1, PEER_TOPK):
        z = z + jnp.exp(best[r] - best[0])
    tau = best[PEER_TOPK - 1][0:1, :]
    floor = jnp.exp((tau - b[0][0:1, :]) - s0)
    rowgate = jnp.exp(s0 - a[0][0:1, :]) / z[0:1, :]
    colgate = jnp.exp(s1 - b[0][0:1, :])
    return floor, rowgate, colgate


def _row(words, k):
    return jnp.broadcast_to(words[k:k + 1, :], (PEER_NKEYS, LANES))


def _peer_gate_row(eb, ii, floor_ref, rowg_ref, colg_ref, tile_ref, gt_ref):
    k = ii % SUBLANES
    i0 = pl.multiple_of(eb * (PEER_EB // PEER_NKEYS) + (ii - k), SUBLANES)
    for c in range(PEER_CHUNKS):
        w = jnp.zeros((PEER_NKEYS, LANES), jnp.float32)
        for h in range(PEER_HEADS):
            colgate = colg_ref[h, c]
            selected = colgate >= _row(floor_ref[h, c, pl.ds(i0, SUBLANES), :], k)
            w = w + jnp.where(selected, colgate, 0.0) * _row(rowg_ref[h, c, pl.ds(i0, SUBLANES), :], k)
        gt_ref[ii * PEER_NKEYS:(ii + 1) * PEER_NKEYS, c * LANES:(c + 1) * LANES] = (w * tile_ref[ii, c]).astype(jnp.bfloat16)


def _peer_kernel(x_ref, g_ref, wqt_ref, sk_ref, u_ref, vt_ref, fg_ref, o_ref,
                 xnt_ref, qt_ref, tile_ref, floor_ref, rowg_ref, colg_ref, gt_ref, acc_ref, *, final_norm):
    eb = pl.program_id(2)
    neb = pl.num_programs(2)

    @pl.when(eb == 0)
    def _prep():
        xn = _rms(x_ref[...], g_ref[...])
        xnt_ref[...] = xn.T.astype(jnp.bfloat16)
        acc_ref[...] = jnp.zeros_like(acc_ref)

        q_rows = 2 * PEER_HEADS * PEER_HALF // PEER_Q_PIECES
        for k in range(PEER_Q_PIECES):
            rows = slice(k * q_rows, (k + 1) * q_rows)
            qt_ref[rows, :] = _dot(wqt_ref[rows, :], xnt_ref[...]).astype(jnp.bfloat16)

        def scores(hp, carry):
            s = _dot(sk_ref[hp], qt_ref[pl.ds(pl.multiple_of(hp * PEER_HALF, PEER_HALF), PEER_HALF), :])
            for c in range(PEER_CHUNKS):
                tile_ref[hp, c] = s[:, c * LANES:(c + 1) * LANES]
            return carry

        lax.fori_loop(0, 2 * PEER_HEADS, scores, 0)

        def chunk_pair(idx, carry):
            h = idx // (PEER_CHUNKS // 2)
            for c in (2 * (idx % (PEER_CHUNKS // 2)), 2 * (idx % (PEER_CHUNKS // 2)) + 1):
                floor_ref[h, c], rowg_ref[h, c], colg_ref[h, c] = _peer_prep_chunk(
                    tile_ref[2 * h, c], tile_ref[2 * h + 1, c])
            return carry

        lax.fori_loop(0, PEER_HEADS * PEER_CHUNKS // 2, chunk_pair, 0)

    for ii in range(PEER_EB // PEER_NKEYS):
        hidden = _dot(u_ref[ii * PEER_NKEYS:(ii + 1) * PEER_NKEYS, :], xnt_ref[...])
        act = 0.5 * hidden * (1.0 + lax.erf(hidden * (2.0 ** -0.5)))
        for c in range(PEER_CHUNKS):
            tile_ref[ii, c] = act[:, c * LANES:(c + 1) * LANES]

    slab_rows = PEER_V_SLAB // PEER_NKEYS
    for kt in range(PEER_EB // PEER_V_SLAB):
        for ii in range(kt * slab_rows, (kt + 1) * slab_rows):
            _peer_gate_row(eb, ii, floor_ref, rowg_ref, colg_ref, tile_ref, gt_ref)
        slab = slice(kt * PEER_V_SLAB, (kt + 1) * PEER_V_SLAB)
        acc_ref[...] += _dot(vt_ref[:, slab], gt_ref[slab, :])

    @pl.when(eb == neb - 1)
    def _finish():
        y = x_ref[...] + acc_ref[...].T
        if final_norm:
            y = _rms(y, fg_ref[...])
        o_ref[...] = y


def _peer(x, g, wqt, sk, u, vt, fg, final_norm):
    b, s, _ = x.shape
    tok = pl.BlockSpec((None, PEER_TOK, D_MODEL), lambda bi, i, e: (bi, i, 0))
    full = lambda shape: pl.BlockSpec(shape, lambda bi, i, e: (0,) * len(shape), pipeline_mode=pl.Buffered(1))
    gate_scratch = pltpu.VMEM((PEER_HEADS, PEER_CHUNKS, PEER_NKEYS, LANES), jnp.float32)
    return pl.pallas_call(
        functools.partial(_peer_kernel, final_norm=final_norm),
        name="peer",
        grid=(b, s // PEER_TOK, PEER_EXPERTS // PEER_EB),
        in_specs=[
            tok, full((1, D_MODEL)), full((2 * PEER_HEADS * PEER_HALF, D_MODEL)),
            full((2 * PEER_HEADS, PEER_NKEYS, PEER_HALF)),
            pl.BlockSpec((PEER_EB, D_MODEL), lambda bi, i, e: (e, 0)),
            pl.BlockSpec((D_MODEL, PEER_EB), lambda bi, i, e: (0, e)),
            full((1, D_MODEL)),
        ],
        out_specs=tok,
        out_shape=jax.ShapeDtypeStruct((b, s, D_MODEL), jnp.float32),
        scratch_shapes=[
            pltpu.VMEM((D_MODEL, PEER_TOK), jnp.bfloat16),
            pltpu.VMEM((2 * PEER_HEADS * PEER_HALF, PEER_TOK), jnp.bfloat16),
            pltpu.VMEM((max(2 * PEER_HEADS, PEER_EB // PEER_NKEYS), PEER_CHUNKS, PEER_NKEYS, LANES), jnp.float32),
            gate_scratch, gate_scratch, gate_scratch,
            pltpu.VMEM((PEER_EB, PEER_TOK), jnp.bfloat16),
            pltpu.VMEM((D_MODEL, PEER_TOK), jnp.float32),
        ],
        compiler_params=_params(("parallel", "parallel", "arbitrary")),
    )(x, g, wqt, sk, u, vt, fg)


def _prepare_layer(l, w_in, conv_w, swa_sink, w_out, norm_mix_g, norm_xa_g, norm_mem_g, w_xq, w_xk, w_xv, w_xo,
                   norm_ffn_g, peer_wq, peer_subkeys, peer_u, peer_v):
    bf = jnp.bfloat16
    scale = HEAD_DIM ** -0.5
    o = np.cumsum([0, NA_DIM, NA_DIM, NA_DIM, SC_DIM, SC_DIM, SC_DIM, SWA_DIM, SWA_KV_DIM, SWA_KV_DIM])
    wi = w_in[l]
    w_in_l = jnp.concatenate([
        wi[:, o[0]:o[1]] * scale, wi[:, o[1]:o[6]], _swa_regroup(wi[:, o[6]:o[7]] * scale, 1), wi[:, o[7]:o[9]],
    ], axis=1).astype(bf)
    wo = w_out[l]
    row = lambda v: v.reshape(1, -1).astype(jnp.float32)
    return dict(
        g_mix=row(norm_mix_g[l]), w_in=w_in_l, conv_w=conv_w[l].astype(jnp.float32),
        sink=swa_sink[l].astype(jnp.float32),
        wna=wo[:NA_DIM].astype(bf), wsc=wo[NA_DIM:NA_DIM + SC_DIM].astype(bf),
        wsw=_swa_regroup(wo[NA_DIM + SC_DIM:], 0).astype(bf),
        g_xa=row(norm_xa_g[l]), g_mem=row(norm_mem_g[l]),
        wxq=w_xq[l].astype(bf), wxk=w_xk[l].astype(bf), wxv=w_xv[l].astype(bf), wxo=w_xo[l].astype(bf),
        g_ffn=row(norm_ffn_g[l]), wqt=peer_wq[l].T.astype(bf),
        sk=peer_subkeys[l].reshape(2 * PEER_HEADS, PEER_NKEYS, PEER_HALF).astype(bf),
        u=peer_u[l].astype(bf), vt=peer_v[l].T.astype(bf),
    )


def _trunk(x, mem, layers, na_bias, swa_bias, final_g):
    depth = len(layers)
    for l, p in enumerate(layers):
        naq, nak, nav, u, bgate, swq, swk, swv = _mix_in(x, p["g_mix"], p["w_in"])
        yna = _na_attn(naq, nak, nav, na_bias[l])
        ysw = _swa_attn(p["sink"], swq, swk, swv, swa_bias)
        km, vm = _mem_kv(mem, p["g_mem"], p["wxk"], p["wxv"])
        x = _mix_out(x, yna, ysw, u, bgate, p["conv_w"], p["wna"], p["wsc"], p["wsw"],
                     p["g_xa"], p["wxq"], km, vm, p["wxo"])
        x = _peer(x, p["g_ffn"], p["wqt"], p["sk"], p["u"], p["vt"], final_g, final_norm=(l == depth - 1))
    return x


def kernel(x_prompt, x_sample, mem_prompt, mem_sample, norm_mix_g, w_in, na_rpb, conv_w, swa_sink, t5_bias, w_out,
           norm_xa_g, norm_mem_g, w_xq, w_xk, w_xv, w_xo, norm_ffn_g, peer_wq, peer_subkeys, peer_u, peer_v, final_g):
    depth = w_in.shape[0]
    layers = [_prepare_layer(l, w_in, conv_w, swa_sink, w_out, norm_mix_g, norm_xa_g, norm_mem_g, w_xq, w_xk, w_xv,
                             w_xo, norm_ffn_g, peer_wq, peer_subkeys, peer_u, peer_v) for l in range(depth)]
    fg = final_g.reshape(1, -1).astype(jnp.float32)
    tables = {}
    outs = []
    for x, mem in ((x_prompt, mem_prompt), (x_sample, mem_sample)):
        s = x.shape[1]
        if s not in tables:
            tables[s] = ([_na_bias_tables(na_rpb[l], s) for l in range(depth)], _swa_bias_tables(t5_bias, s))
        outs.append(_trunk(x, mem, layers, *tables[s], fg))
    return tuple(outs)
```

```python
import functools

import numpy as np
import jax
import jax.numpy as jnp
from jax import lax
from jax.experimental import pallas as pl
from jax.experimental.pallas import tpu as pltpu

D_MODEL = 1024
GRID_W = 64
HEAD_DIM = 64
NA_HEADS = 6
NA_WIN_H = 8
NA_WIN_W = 16
NA_DIM = NA_HEADS * HEAD_DIM
SC_DIM = 256
SWA_HEADS = 6
SWA_KV_HEADS = 2
SWA_GROUP = SWA_HEADS // SWA_KV_HEADS
SWA_WINDOW = 128
SWA_DIM = SWA_HEADS * HEAD_DIM
SWA_KV_DIM = SWA_KV_HEADS * HEAD_DIM
T5_BUCKETS = 32
T5_MAX_DIST = 128
IN_DIM = 3 * NA_DIM + 3 * SC_DIM + SWA_DIM + 2 * SWA_KV_DIM
XA_HEADS = 4
XA_HEAD_DIM = 128
XA_DIM = XA_HEADS * XA_HEAD_DIM
PEER_HEADS = 8
PEER_NKEYS = 128
PEER_EXPERTS = PEER_NKEYS * PEER_NKEYS
PEER_TOPK = 16
PEER_HALF = 128
RMS_EPS = 1e-6
NEG_INF = -1e30

LANES = 128
SUBLANES = 8
VMEM_LIMIT = 56 * 1024 * 1024

TOK_TILE = 512
NA_Q_ROWS = 4
NA_K_ROWS = NA_Q_ROWS + NA_WIN_H
NA_Q = NA_Q_ROWS * GRID_W
NA_K = NA_K_ROWS * GRID_W
SWA_Q = 256
SWA_K = SWA_Q + 2 * SWA_WINDOW
PEER_TOK = 512
PEER_EB = 2048
PEER_CHUNKS = PEER_TOK // LANES
PEER_V_SLAB = 256
PEER_Q_PIECES = 4

_NT = (((1,), (1,)), ((), ()))


def _rms(xf, g):
    return xf * lax.rsqrt(jnp.mean(xf * xf, axis=-1, keepdims=True) + RMS_EPS) * g


def _dot(a, b):
    return jnp.dot(a, b, preferred_element_type=jnp.float32)


def _dot_nt(a, b):
    return lax.dot_general(a, b, _NT, preferred_element_type=jnp.float32)


def _params(sem, flags=None):
    return pltpu.CompilerParams(dimension_semantics=sem, vmem_limit_bytes=VMEM_LIMIT, flags=flags)


def _mix_in_kernel(x_ref, g_ref, w_ref, naq_ref, nak_ref, nav_ref, u_ref, b_ref, swq_ref, swk_ref, swv_ref):
    xn = _rms(x_ref[...], g_ref[...]).astype(jnp.bfloat16)
    z = _dot(xn, w_ref[...])
    o = 0
    naq_ref[...] = z[:, o:o + NA_DIM].astype(jnp.bfloat16); o += NA_DIM
    nak_ref[...] = z[:, o:o + NA_DIM].astype(jnp.bfloat16); o += NA_DIM
    nav_ref[...] = z[:, o:o + NA_DIM].astype(jnp.bfloat16); o += NA_DIM
    b_ref[...] = z[:, o:o + SC_DIM]; o += SC_DIM
    u_ref[...] = z[:, o:o + SC_DIM] * z[:, o + SC_DIM:o + 2 * SC_DIM]; o += 2 * SC_DIM
    swq_ref[...] = z[:, o:o + SWA_DIM].astype(jnp.bfloat16); o += SWA_DIM
    swk_ref[...] = z[:, o:o + SWA_KV_DIM].astype(jnp.bfloat16); o += SWA_KV_DIM
    swv_ref[...] = z[:, o:o + SWA_KV_DIM].astype(jnp.bfloat16)


def _mix_in(x, g, w):
    b, s, _ = x.shape
    tok = lambda d: pl.BlockSpec((None, TOK_TILE, d), lambda bi, i: (bi, i, 0))
    full = lambda shape: pl.BlockSpec(shape, lambda bi, i: (0,) * len(shape))
    widths = [(NA_DIM, jnp.bfloat16)] * 3 + [(SC_DIM, jnp.float32)] * 2 + \
             [(SWA_DIM, jnp.bfloat16), (SWA_KV_DIM, jnp.bfloat16), (SWA_KV_DIM, jnp.bfloat16)]
    return pl.pallas_call(
        _mix_in_kernel,
        name="mix_in",
        grid=(b, s // TOK_TILE),
        in_specs=[tok(D_MODEL), full((1, D_MODEL)), full((D_MODEL, IN_DIM))],
        out_specs=[tok(d) for d, _ in widths],
        out_shape=[jax.ShapeDtypeStruct((b, s, d), t) for d, t in widths],
        compiler_params=_params(("parallel", "parallel")),
    )(x, g, w)


def _pair_heads(q_pair, k_pair, v_pair, bias_lo, bias_hi, extra_lo=None, extra_hi=None):
    low = lax.broadcasted_iota(jnp.int32, (1, LANES), 1) < HEAD_DIM
    outs = []
    for keep, bias, extra in ((low, bias_lo, extra_lo), (~low, bias_hi, extra_hi)):
        qm = jnp.where(keep, q_pair, jnp.zeros_like(q_pair))
        logits = _dot_nt(qm, k_pair) + bias
        mx = jnp.max(logits, axis=-1, keepdims=True)
        if extra is not None:
            mx = jnp.maximum(mx, extra)
        p = jnp.exp(logits - mx)
        den = jnp.sum(p, axis=-1, keepdims=True)
        if extra is not None:
            den = den + jnp.exp(extra - mx)
        outs.append(_dot(p.astype(jnp.bfloat16), v_pair) / den)
    return jnp.where(low, outs[0], outs[1])


def _na_kernel(q_ref, k_ref, v_ref, bias_ref, o_ref):
    m = pl.program_id(1)
    rows = k_ref.shape[0] // GRID_W
    start_row = jnp.clip(m * NA_Q_ROWS - NA_WIN_H // 2, 0, rows - NA_K_ROWS)
    start = pl.multiple_of(start_row * GRID_W, GRID_W * NA_Q_ROWS)
    kwin = k_ref[pl.ds(start, NA_K), :]
    vwin = v_ref[pl.ds(start, NA_K), :]
    q = q_ref[...]
    for g in range(NA_HEADS // 2):
        sl = slice(g * LANES, (g + 1) * LANES)
        out = _pair_heads(q[:, sl], kwin[:, sl], vwin[:, sl], bias_ref[2 * g], bias_ref[2 * g + 1])
        o_ref[:, sl] = out.astype(o_ref.dtype)


def _na_case(m, nsteps):
    return jnp.where(m == 0, 0, jnp.where(m == nsteps - 1, 2, 1))


def _na_attn(q, k, v, bias):
    b, s, _ = q.shape
    nsteps = s // NA_Q
    return pl.pallas_call(
        _na_kernel,
        name="na_attn",
        grid=(b, nsteps),
        in_specs=[
            pl.BlockSpec((None, NA_Q, NA_DIM), lambda bi, m: (bi, m, 0)),
            pl.BlockSpec((None, s, NA_DIM), lambda bi, m: (bi, 0, 0)),
            pl.BlockSpec((None, s, NA_DIM), lambda bi, m: (bi, 0, 0)),
            pl.BlockSpec((None, NA_HEADS, NA_Q, NA_K), lambda bi, m: (_na_case(m, nsteps), 0, 0, 0)),
        ],
        out_specs=pl.BlockSpec((None, NA_Q, NA_DIM), lambda bi, m: (bi, m, 0)),
        out_shape=jax.ShapeDtypeStruct((b, s, NA_DIM), jnp.bfloat16),
        compiler_params=_params(("parallel", "arbitrary")),
    )(q, k, v, bias)


def _na_bias_tables(rpb, s):
    rows = s // GRID_W
    nsteps = rows // NA_Q_ROWS

    c = np.arange(GRID_W)[:, None]
    kc = np.arange(GRID_W)[None, :]
    c0 = np.clip(c - NA_WIN_W // 2, 0, GRID_W - NA_WIN_W)
    valid_c = (kc >= c0) & (kc < c0 + NA_WIN_W)
    sel_c = (kc - c + NA_WIN_W - 1)[..., None] == np.arange(2 * NA_WIN_W - 1)

    def row_geometry(m):
        start_row = int(np.clip(m * NA_Q_ROWS - NA_WIN_H // 2, 0, rows - NA_K_ROWS))
        r = (m * NA_Q_ROWS + np.arange(NA_Q_ROWS))[:, None]
        kr = (start_row + np.arange(NA_K_ROWS))[None, :]
        r0 = np.clip(r - NA_WIN_H // 2, 0, rows - NA_WIN_H)
        valid_r = (kr >= r0) & (kr < r0 + NA_WIN_H)
        sel_r = (kr - r + NA_WIN_H - 1)[..., None] == np.arange(2 * NA_WIN_H - 1)
        return valid_r, sel_r

    interior = row_geometry(1)
    for m in range(1, nsteps - 1):
        assert all(np.array_equal(a, b_) for a, b_ in zip(row_geometry(m), interior))
    tabs = []
    for valid_r, sel_r in (row_geometry(0), interior, row_geometry(nsteps - 1)):
        t = jnp.einsum("rka,abh,cjb->hrckj", sel_r.astype(np.float32), rpb.astype(jnp.float32),
                       sel_c.astype(np.float32), precision=lax.Precision.HIGHEST)
        valid = valid_r[:, None, :, None] & valid_c[None, :, None, :]
        tabs.append(jnp.where(valid[None], t, NEG_INF).reshape(NA_HEADS, NA_Q, NA_K))
    return jnp.stack(tabs)


def _swa_kernel(sink_ref, q_ref, k_ref, v_ref, bias_ref, o_ref):
    n = pl.program_id(1)
    s = k_ref.shape[0]
    start = pl.multiple_of(jnp.clip(n * SWA_Q - SWA_WINDOW, 0, s - SWA_K), SWA_WINDOW)
    kwin = k_ref[pl.ds(start, SWA_K), :]
    vwin = v_ref[pl.ds(start, SWA_K), :]
    q = q_ref[...]
    for g in range(SWA_GROUP):
        sl = slice(g * LANES, (g + 1) * LANES)
        out = _pair_heads(q[:, sl], kwin, vwin, bias_ref[2 * g], bias_ref[2 * g + 1],
                          sink_ref[g], sink_ref[g + SWA_GROUP])
        o_ref[:, sl] = out.astype(o_ref.dtype)


def _swa_attn(sink, q, k, v, bias):
    b, s, _ = q.shape
    nsteps = s // SWA_Q
    return pl.pallas_call(
        _swa_kernel,
        name="swa_attn",
        grid=(b, nsteps),
        in_specs=[
            pl.BlockSpec(memory_space=pltpu.SMEM),
            pl.BlockSpec((None, SWA_Q, SWA_DIM), lambda bi, n: (bi, n, 0)),
            pl.BlockSpec((None, s, SWA_KV_DIM), lambda bi, n: (bi, 0, 0)),
            pl.BlockSpec((None, s, SWA_KV_DIM), lambda bi, n: (bi, 0, 0)),
            pl.BlockSpec((None, SWA_HEADS, SWA_Q, SWA_K), lambda bi, n: (_na_case(n, nsteps), 0, 0, 0)),
        ],
        out_specs=pl.BlockSpec((None, SWA_Q, SWA_DIM), lambda bi, n: (bi, n, 0)),
        out_shape=jax.ShapeDtypeStruct((b, s, SWA_DIM), jnp.bfloat16),
        compiler_params=_params(("parallel", "arbitrary")),
    )(sink, q, k, v, bias)


def _t5_bucket(rel):
    nb = T5_BUCKETS // 2
    max_exact = nb // 2
    ret = (rel > 0).astype(np.int32) * nb
    n = np.abs(rel)
    large = max_exact + (np.log(np.maximum(n, 1) / max_exact) / np.log(T5_MAX_DIST / max_exact)
                         * (nb - max_exact)).astype(np.int32)
    large = np.minimum(large, nb - 1)
    return (ret + np.where(n < max_exact, n, large)).astype(np.int32)


_SWA_SLOT_HEADS = [h for g in range(SWA_GROUP) for h in (g, g + SWA_GROUP)]


def _swa_regroup(w, axis):
    blocks = [lax.slice_in_dim(w, h * HEAD_DIM, (h + 1) * HEAD_DIM, axis=axis) for h in _SWA_SLOT_HEADS]
    return jnp.concatenate(blocks, axis=axis)


def _swa_bias_tables(t5_bias, s):
    offsets = np.arange(-SWA_WINDOW, SWA_WINDOW + 1)
    sel = _t5_bucket(offsets)[:, None] == np.arange(T5_BUCKETS)
    bias_off = jnp.dot(sel.astype(np.float32), t5_bias.astype(jnp.float32), precision=lax.Precision.HIGHEST)
    bias_off = jnp.stack([bias_off[:, h] for h in _SWA_SLOT_HEADS])
    nsteps = s // SWA_Q
    period = SWA_Q + SWA_K - 1
    tabs = []
    for n in (0, 1, nsteps - 1):
        start = int(np.clip(n * SWA_Q - SWA_WINDOW, 0, s - SWA_K))
        delta = n * SWA_Q - start
        lead = SWA_Q - 1 + delta - SWA_WINDOW
        g = jnp.concatenate([jnp.full((SWA_HEADS, lead), NEG_INF, jnp.float32), bias_off,
                             jnp.full((SWA_HEADS, period - lead - 2 * SWA_WINDOW - 1), NEG_INF, jnp.float32)], axis=1)
        hankel = jnp.tile(g, (1, SWA_Q + 1))[:, :SWA_Q * (period + 1)].reshape(SWA_HEADS, SWA_Q, period + 1)
        tabs.append(jnp.flip(hankel[:, :, :SWA_K], axis=1))
    return jnp.stack(tabs)


def _mem_kv_kernel(mem_ref, g_ref, wk_ref, wv_ref, k_ref, v_ref):
    mn = _rms(mem_ref[...], g_ref[...]).astype(jnp.bfloat16)
    k_ref[...] = _dot(mn, wk_ref[...]).astype(jnp.bfloat16)
    v_ref[...] = _dot(mn, wv_ref[...]).astype(jnp.bfloat16)


def _mem_kv(mem, g, wk, wv):
    b, m, _ = mem.shape
    full = lambda shape: pl.BlockSpec(shape, lambda bi: (0,) * len(shape))
    return pl.pallas_call(
        _mem_kv_kernel,
        name="mem_kv",
        grid=(b,),
        in_specs=[pl.BlockSpec((None, m, D_MODEL), lambda bi: (bi, 0, 0)), full((1, D_MODEL)),
                  full((D_MODEL, XA_DIM)), full((D_MODEL, XA_DIM))],
        out_specs=[pl.BlockSpec((None, m, XA_DIM), lambda bi: (bi, 0, 0))] * 2,
        out_shape=[jax.ShapeDtypeStruct((b, m, XA_DIM), jnp.bfloat16)] * 2,
        compiler_params=_params(("parallel",)),
    )(mem, g, wk, wv)


def _mix_out_kernel(x_ref, yna_ref, ysw_ref, u_ref, uprev_ref, unext_ref, b_ref, cw_ref,
                    wna_ref, wsc_ref, wsw_ref, g_ref, wq_ref, km_ref, vm_ref, wo_ref, o_ref):
    i = pl.program_id(1)
    nt = pl.num_programs(1)
    u = u_ref[...]
    t = u.shape[0]
    row = lax.broadcasted_iota(jnp.int32, (t, 1), 0)
    prev_row = jnp.where(i > 0, uprev_ref[SUBLANES - 1:SUBLANES, :], 0.0)
    next_row = jnp.where(i < nt - 1, unext_ref[0:1, :], 0.0)
    u_m1 = jnp.where(row == 0, prev_row, pltpu.roll(u, 1, axis=0))
    u_p1 = jnp.where(row == t - 1, next_row, pltpu.roll(u, t - 1, axis=0))
    ysc = b_ref[...] * (u_m1 * cw_ref[0:1, :] + u * cw_ref[1:2, :] + u_p1 * cw_ref[2:3, :])
    y = _dot(yna_ref[...], wna_ref[...]) + _dot(ysc.astype(jnp.bfloat16), wsc_ref[...]) \
        + _dot(ysw_ref[...], wsw_ref[...])
    x1 = x_ref[...] + y

    xn = _rms(x1, g_ref[...]).astype(jnp.bfloat16)
    q = _dot(xn, wq_ref[...])
    outs = []
    for h in range(XA_HEADS):
        sl = slice(h * XA_HEAD_DIM, (h + 1) * XA_HEAD_DIM)
        logits = _dot_nt(q[:, sl].astype(jnp.bfloat16), km_ref[:, sl]) * (XA_HEAD_DIM ** -0.5)
        mx = jnp.max(logits, axis=-1, keepdims=True)
        p = jnp.exp(logits - mx)
        den = jnp.sum(p, axis=-1, keepdims=True)
        outs.append((_dot(p.astype(jnp.bfloat16), vm_ref[:, sl]) / den).astype(jnp.bfloat16))
    o = jnp.concatenate(outs, axis=-1)
    o_ref[...] = x1 + _dot(o, wo_ref[...])


def _mix_out(x, yna, ysw, u, bgate, cw, wna, wsc, wsw, g, wq, km, vm, wo):
    b, s, _ = x.shape
    nt = s // TOK_TILE
    halo_blocks = TOK_TILE // SUBLANES
    last_halo = s // SUBLANES - 1
    tok = lambda d: pl.BlockSpec((None, TOK_TILE, d), lambda bi, i: (bi, i, 0))
    full = lambda shape: pl.BlockSpec(shape, lambda bi, i: (0,) * len(shape))
    mem = pl.BlockSpec((None, km.shape[1], XA_DIM), lambda bi, i: (bi, 0, 0))
    return pl.pallas_call(
        _mix_out_kernel,
        name="mix_out",
        grid=(b, nt),
        in_specs=[
            tok(D_MODEL), tok(NA_DIM), tok(SWA_DIM), tok(SC_DIM),
            pl.BlockSpec((None, SUBLANES, SC_DIM), lambda bi, i: (bi, jnp.maximum(i * halo_blocks - 1, 0), 0)),
            pl.BlockSpec((None, SUBLANES, SC_DIM), lambda bi, i: (bi, jnp.minimum((i + 1) * halo_blocks, last_halo), 0)),
            tok(SC_DIM), full((3, SC_DIM)),
            full((NA_DIM, D_MODEL)), full((SC_DIM, D_MODEL)), full((SWA_DIM, D_MODEL)),
            full((1, D_MODEL)), full((D_MODEL, XA_DIM)), mem, mem, full((XA_DIM, D_MODEL)),
        ],
        out_specs=tok(D_MODEL),
        out_shape=jax.ShapeDtypeStruct((b, s, D_MODEL), jnp.float32),
        compiler_params=_params(("parallel", "parallel")),
    )(x, yna, ysw, u, u, u, bgate, cw, wna, wsc, wsw, g, wq, km, vm, wo)


_CAND_PAIRS = [(p, q) for p in range(PEER_TOPK) for q in range(PEER_TOPK) if (p + 1) * (q + 1) <= PEER_TOPK]
_CAND_VREGS = -(-len(_CAND_PAIRS) // SUBLANES)


def _allmax_sublanes(v):
    for shift in (4, 2, 1):
        v = jnp.maximum(v, pltpu.roll(v, shift, axis=0))
    return v


def _top_values(s3, count):
    vals = []
    for r in range(count):
        m = _allmax_sublanes(jnp.max(s3, axis=0))
        vals.append(m)
        if r + 1 < count:
            s3 = jnp.where(s3 == m[None], -jnp.inf, s3)
    return vals


def _batcher_pairs(n):
    pairs = []
    p = 1
    while p < n:
        k = p
        while k >= 1:
            for j in range(k % p, n - k, 2 * k):
                for i in range(min(k, n - j - k)):
                    if (i + j) // (2 * p) == (i + j + k) // (2 * p):
                        pairs.append((i + j, i + j + k))
            k //= 2
        p *= 2
    return pairs


_SORT16 = _batcher_pairs(PEER_TOPK)
_BITONIC16 = [(i, i + k) for k in (8, 4, 2, 1) for i in range(PEER_TOPK) if not i & k]


def _descending(vals, pairs):
    vals = list(vals)
    for i, j in pairs:
        vals[i], vals[j] = jnp.maximum(vals[i], vals[j]), jnp.minimum(vals[i], vals[j])
    return vals


def _top16_sorted(s3):
    vals = _descending([s3[k] for k in range(PEER_TOPK)], _SORT16)
    for shift in (4, 2, 1):
        other = [pltpu.roll(v, shift, axis=0) for v in vals]
        vals = _descending([jnp.maximum(vals[k], other[PEER_TOPK - 1 - k]) for k in range(PEER_TOPK)], _BITONIC16)
    return vals


def _peer_prep_chunk(s0, s1):
    groups = PEER_NKEYS // SUBLANES
    a = _top16_sorted(s0.reshape(groups, SUBLANES, LANES))
    b = _top16_sorted(s1.reshape(groups, SUBLANES, LANES))
    sub = lax.broadcasted_iota(jnp.int32, (SUBLANES, LANES), 0)
    cand = []
    for v in range(_CAND_VREGS):
        acc = jnp.full((SUBLANES, LANES), -jnp.inf, jnp.float32)
        for k, (p, q) in enumerate(_CAND_PAIRS[v * SUBLANES:(v + 1) * SUBLANES]):
            acc = jnp.where(sub == k, a[p] + b[q], acc)
        cand.append(acc)
    best = _top_values(jnp.stack(cand), PEER_TOPK)
    z = jnp.ones((SUBLANES, LANES), jnp.float32)
    for r in range(1, PEER_TOPK):
        z = z + jnp.exp(best[r] - best[0])
    tau = best[PEER_TOPK - 1][0:1, :]
    floor = jnp.exp((tau - b[0][0:1, :]) - s0)
    rowgate = jnp.exp(s0 - a[0][0:1, :]) * (0.5 / z[0:1, :])
    colgate = jnp.exp(s1 - b[0][0:1, :])
    return floor, rowgate, colgate


def _row(words, k):
    return jnp.broadcast_to(words[k:k + 1, :], (PEER_NKEYS, LANES))


def _peer_gate_row(eb, ii, floor_ref, rowg_ref, colg_ref, tile_ref, gt_ref):
    k = ii % SUBLANES
    i0 = pl.multiple_of(eb * (PEER_EB // PEER_NKEYS) + (ii - k), SUBLANES)
    for c in range(PEER_CHUNKS):
        w = jnp.zeros((PEER_NKEYS, LANES), jnp.float32)
        for h in range(PEER_HEADS):
            colgate = colg_ref[h, c]
            selected = colgate >= _row(floor_ref[h, c, pl.ds(i0, SUBLANES), :], k)
            w = w + jnp.where(selected, colgate, 0.0) * _row(rowg_ref[h, c, pl.ds(i0, SUBLANES), :], k)
        gt_ref[ii * PEER_NKEYS:(ii + 1) * PEER_NKEYS, c * LANES:(c + 1) * LANES] = (w * tile_ref[ii, c]).astype(jnp.bfloat16)


def _peer_kernel(x_ref, g_ref, wqt_ref, sk_ref, u_ref, vt_ref, fg_ref, o_ref,
                 xnt_ref, qt_ref, tile_ref, floor_ref, rowg_ref, colg_ref, gt_ref, acc_ref, *, final_norm):
    eb = pl.program_id(2)
    neb = pl.num_programs(2)

    @pl.when(eb == 0)
    def _prep():
        xn = _rms(x_ref[...], g_ref[...])
        xnt_ref[...] = xn.T.astype(jnp.bfloat16)
        acc_ref[...] = jnp.zeros_like(acc_ref)

        q_rows = 2 * PEER_HEADS * PEER_HALF // PEER_Q_PIECES
        for k in range(PEER_Q_PIECES):
            rows = slice(k * q_rows, (k + 1) * q_rows)
            qt_ref[rows, :] = _dot(wqt_ref[rows, :], xnt_ref[...]).astype(jnp.bfloat16)

        for hp in range(2 * PEER_HEADS):
            s = _dot(sk_ref[hp], qt_ref[hp * PEER_HALF:(hp + 1) * PEER_HALF, :])
            for c in range(PEER_CHUNKS):
                tile_ref[hp, c] = s[:, c * LANES:(c + 1) * LANES]

        def chunk_pair(idx, carry):
            h = idx // (PEER_CHUNKS // 2)
            for c in (2 * (idx % (PEER_CHUNKS // 2)), 2 * (idx % (PEER_CHUNKS // 2)) + 1):
                floor_ref[h, c], rowg_ref[h, c], colg_ref[h, c] = _peer_prep_chunk(
                    tile_ref[2 * h, c], tile_ref[2 * h + 1, c])
            return carry

        lax.fori_loop(0, PEER_HEADS * PEER_CHUNKS // 2, chunk_pair, 0)

    slab_rows = PEER_V_SLAB // PEER_NKEYS
    n_slabs = PEER_EB // PEER_V_SLAB

    for ii in range(PEER_EB // PEER_NKEYS):
        hidden = _dot(u_ref[ii * PEER_NKEYS:(ii + 1) * PEER_NKEYS, :], xnt_ref[...])
        act = hidden + hidden * lax.erf(hidden * (2.0 ** -0.5))
        for c in range(PEER_CHUNKS):
            tile_ref[ii, c] = act[:, c * LANES:(c + 1) * LANES]

    for kt in range(n_slabs):
        for ii in range(kt * slab_rows, (kt + 1) * slab_rows):
            _peer_gate_row(eb, ii, floor_ref, rowg_ref, colg_ref, tile_ref, gt_ref)
        slab = slice(kt * PEER_V_SLAB, (kt + 1) * PEER_V_SLAB)
        acc_ref[...] += _dot(vt_ref[:, slab], gt_ref[slab, :])

    @pl.when(eb == neb - 1)
    def _finish():
        y = x_ref[...] + acc_ref[...].T
        if final_norm:
            y = _rms(y, fg_ref[...])
        o_ref[...] = y


def _peer(x, g, wqt, sk, u, vt, fg, final_norm):
    b, s, _ = x.shape
    tok = pl.BlockSpec((None, PEER_TOK, D_MODEL), lambda bi, i, e: (bi, i, 0))
    full = lambda shape: pl.BlockSpec(shape, lambda bi, i, e: (0,) * len(shape), pipeline_mode=pl.Buffered(1))
    gate_scratch = pltpu.VMEM((PEER_HEADS, PEER_CHUNKS, PEER_NKEYS, LANES), jnp.float32)
    return pl.pallas_call(
        functools.partial(_peer_kernel, final_norm=final_norm),
        name="peer",
        grid=(b, s // PEER_TOK, PEER_EXPERTS // PEER_EB),
        in_specs=[
            tok, full((1, D_MODEL)), full((2 * PEER_HEADS * PEER_HALF, D_MODEL)),
            full((2 * PEER_HEADS, PEER_NKEYS, PEER_HALF)),
            pl.BlockSpec((PEER_EB, D_MODEL), lambda bi, i, e: (e, 0)),
            pl.BlockSpec((D_MODEL, PEER_EB), lambda bi, i, e: (0, e)),
            full((1, D_MODEL)),
        ],
        out_specs=tok,
        out_shape=jax.ShapeDtypeStruct((b, s, D_MODEL), jnp.float32),
        scratch_shapes=[
            pltpu.VMEM((D_MODEL, PEER_TOK), jnp.bfloat16),
            pltpu.VMEM((2 * PEER_HEADS * PEER_HALF, PEER_TOK), jnp.bfloat16),
            pltpu.VMEM((max(2 * PEER_HEADS, PEER_EB // PEER_NKEYS), PEER_CHUNKS, PEER_NKEYS, LANES), jnp.float32),
            gate_scratch, gate_scratch, gate_scratch,
            pltpu.VMEM((PEER_EB, PEER_TOK), jnp.bfloat16),
            pltpu.VMEM((D_MODEL, PEER_TOK), jnp.float32),
        ],
        compiler_params=_params(("parallel", "parallel", "arbitrary")),
    )(x, g, wqt, sk, u, vt, fg)


def _prepare_layer(l, w_in, conv_w, swa_sink, w_out, norm_mix_g, norm_xa_g, norm_mem_g, w_xq, w_xk, w_xv, w_xo,
                   norm_ffn_g, peer_wq, peer_subkeys, peer_u, peer_v):
    bf = jnp.bfloat16
    scale = HEAD_DIM ** -0.5
    o = np.cumsum([0, NA_DIM, NA_DIM, NA_DIM, SC_DIM, SC_DIM, SC_DIM, SWA_DIM, SWA_KV_DIM, SWA_KV_DIM])
    wi = w_in[l]
    w_in_l = jnp.concatenate([
        wi[:, o[0]:o[1]] * scale, wi[:, o[1]:o[6]], _swa_regroup(wi[:, o[6]:o[7]] * scale, 1), wi[:, o[7]:o[9]],
    ], axis=1).astype(bf)
    wo = w_out[l]
    row = lambda v: v.reshape(1, -1).astype(jnp.float32)
    return dict(
        g_mix=row(norm_mix_g[l]), w_in=w_in_l, conv_w=conv_w[l].astype(jnp.float32),
        sink=swa_sink[l].astype(jnp.float32),
        wna=wo[:NA_DIM].astype(bf), wsc=wo[NA_DIM:NA_DIM + SC_DIM].astype(bf),
        wsw=_swa_regroup(wo[NA_DIM + SC_DIM:], 0).astype(bf),
        g_xa=row(norm_xa_g[l]), g_mem=row(norm_mem_g[l]),
        wxq=w_xq[l].astype(bf), wxk=w_xk[l].astype(bf), wxv=w_xv[l].astype(bf), wxo=w_xo[l].astype(bf),
        g_ffn=row(norm_ffn_g[l]), wqt=peer_wq[l].T.astype(bf),
        sk=peer_subkeys[l].reshape(2 * PEER_HEADS, PEER_NKEYS, PEER_HALF).astype(bf),
        u=peer_u[l].astype(bf), vt=peer_v[l].T.astype(bf),
    )


def _trunk(x, mem, layers, na_bias, swa_bias, final_g):
    depth = len(layers)
    for l, p in enumerate(layers):
        naq, nak, nav, u, bgate, swq, swk, swv = _mix_in(x, p["g_mix"], p["w_in"])
        yna = _na_attn(naq, nak, nav, na_bias[l])
        ysw = _swa_attn(p["sink"], swq, swk, swv, swa_bias)
        km, vm = _mem_kv(mem, p["g_mem"], p["wxk"], p["wxv"])
        x = _mix_out(x, yna, ysw, u, bgate, p["conv_w"], p["wna"], p["wsc"], p["wsw"],
                     p["g_xa"], p["wxq"], km, vm, p["wxo"])
        x = _peer(x, p["g_ffn"], p["wqt"], p["sk"], p["u"], p["vt"], final_g, final_norm=(l == depth - 1))
    return x


def kernel(x_prompt, x_sample, mem_prompt, mem_sample, norm_mix_g, w_in, na_rpb, conv_w, swa_sink, t5_bias, w_out,
           norm_xa_g, norm_mem_g, w_xq, w_xk, w_xv, w_xo, norm_ffn_g, peer_wq, peer_subkeys, peer_u, peer_v, final_g):
    depth = w_in.shape[0]
    layers = [_prepare_layer(l, w_in, conv_w, swa_sink, w_out, norm_mix_g, norm_xa_g, norm_mem_g, w_xq, w_xk, w_xv,
                             w_xo, norm_ffn_g, peer_wq, peer_subkeys, peer_u, peer_v) for l in range(depth)]
    fg = final_g.reshape(1, -1).astype(jnp.float32)
    tables = {}
    outs = []
    for x, mem in ((x_prompt, mem_prompt), (x_sample, mem_sample)):
        s = x.shape[1]
        if s not in tables:
            tables[s] = ([_na_bias_tables(na_rpb[l], s) for l in range(depth)], _swa_bias_tables(t5_bias, s))
        outs.append(_trunk(x, mem, layers, *tables[s], fg))
    return tuple(outs)
```

```python
import functools

import numpy as np
import jax
import jax.numpy as jnp
from jax import lax
from jax.experimental import pallas as pl
from jax.experimental.pallas import tpu as pltpu

D_MODEL = 1024
GRID_W = 64
HEAD_DIM = 64
NA_HEADS = 6
NA_WIN_H = 8
NA_WIN_W = 16
NA_DIM = NA_HEADS * HEAD_DIM
SC_DIM = 256
SWA_HEADS = 6
SWA_KV_HEADS = 2
SWA_GROUP = SWA_HEADS // SWA_KV_HEADS
SWA_WINDOW = 128
SWA_DIM = SWA_HEADS * HEAD_DIM
SWA_KV_DIM = SWA_KV_HEADS * HEAD_DIM
T5_BUCKETS = 32
T5_MAX_DIST = 128
IN_DIM = 3 * NA_DIM + 3 * SC_DIM + SWA_DIM + 2 * SWA_KV_DIM
XA_HEADS = 4
XA_HEAD_DIM = 128
XA_DIM = XA_HEADS * XA_HEAD_DIM
PEER_HEADS = 8
PEER_NKEYS = 128
PEER_EXPERTS = PEER_NKEYS * PEER_NKEYS
PEER_TOPK = 16
PEER_HALF = 128
RMS_EPS = 1e-6
NEG_INF = -1e30

LANES = 128
SUBLANES = 8
VMEM_LIMIT = 56 * 1024 * 1024

TOK_TILE = 512
NA_Q_ROWS = 4
NA_K_ROWS = NA_Q_ROWS + NA_WIN_H
NA_Q = NA_Q_ROWS * GRID_W
NA_K = NA_K_ROWS * GRID_W
SWA_Q = 256
SWA_K = SWA_Q + 2 * SWA_WINDOW
PEER_TOK = 512
PEER_EB = 2048
PEER_CHUNKS = PEER_TOK // LANES
PEER_GATE_ROWS = 2
PEER_Q_PIECES = 4

_NT = (((1,), (1,)), ((), ()))


def _rms(xf, g):
    return xf * lax.rsqrt(jnp.mean(xf * xf, axis=-1, keepdims=True) + RMS_EPS) * g


def _dot(a, b):
    return jnp.dot(a, b, preferred_element_type=jnp.float32)


def _dot_nt(a, b):
    return lax.dot_general(a, b, _NT, preferred_element_type=jnp.float32)


def _params(sem, flags=None):
    return pltpu.CompilerParams(dimension_semantics=sem, vmem_limit_bytes=VMEM_LIMIT, flags=flags)


def _mix_in_kernel(x_ref, g_ref, w_ref, naq_ref, nak_ref, nav_ref, u_ref, b_ref, swq_ref, swk_ref, swv_ref):
    xn = _rms(x_ref[...], g_ref[...]).astype(jnp.bfloat16)
    z = _dot(xn, w_ref[...])
    o = 0
    naq_ref[...] = z[:, o:o + NA_DIM].astype(jnp.bfloat16); o += NA_DIM
    nak_ref[...] = z[:, o:o + NA_DIM].astype(jnp.bfloat16); o += NA_DIM
    nav_ref[...] = z[:, o:o + NA_DIM].astype(jnp.bfloat16); o += NA_DIM
    b_ref[...] = z[:, o:o + SC_DIM]; o += SC_DIM
    u_ref[...] = z[:, o:o + SC_DIM] * z[:, o + SC_DIM:o + 2 * SC_DIM]; o += 2 * SC_DIM
    swq_ref[...] = z[:, o:o + SWA_DIM].astype(jnp.bfloat16); o += SWA_DIM
    swk_ref[...] = z[:, o:o + SWA_KV_DIM].astype(jnp.bfloat16); o += SWA_KV_DIM
    swv_ref[...] = z[:, o:o + SWA_KV_DIM].astype(jnp.bfloat16)


def _mix_in(x, g, w):
    b, s, _ = x.shape
    tok = lambda d: pl.BlockSpec((None, TOK_TILE, d), lambda bi, i: (bi, i, 0))
    full = lambda shape: pl.BlockSpec(shape, lambda bi, i: (0,) * len(shape))
    widths = [(NA_DIM, jnp.bfloat16)] * 3 + [(SC_DIM, jnp.float32)] * 2 + \
             [(SWA_DIM, jnp.bfloat16), (SWA_KV_DIM, jnp.bfloat16), (SWA_KV_DIM, jnp.bfloat16)]
    return pl.pallas_call(
        _mix_in_kernel,
        name="mix_in",
        grid=(b, s // TOK_TILE),
        in_specs=[tok(D_MODEL), full((1, D_MODEL)), full((D_MODEL, IN_DIM))],
        out_specs=[tok(d) for d, _ in widths],
        out_shape=[jax.ShapeDtypeStruct((b, s, d), t) for d, t in widths],
        compiler_params=_params(("parallel", "parallel")),
    )(x, g, w)


def _pair_heads(q_pair, k_pair, v_pair, bias_lo, bias_hi, extra_lo=None, extra_hi=None):
    low = lax.broadcasted_iota(jnp.int32, (1, LANES), 1) < HEAD_DIM
    outs = []
    for keep, bias, extra in ((low, bias_lo, extra_lo), (~low, bias_hi, extra_hi)):
        qm = jnp.where(keep, q_pair, jnp.zeros_like(q_pair))
        logits = _dot_nt(qm, k_pair) + bias
        mx = jnp.max(logits, axis=-1, keepdims=True)
        if extra is not None:
            mx = jnp.maximum(mx, extra)
        p = jnp.exp(logits - mx)
        den = jnp.sum(p, axis=-1, keepdims=True)
        if extra is not None:
            den = den + jnp.exp(extra - mx)
        outs.append(_dot(p.astype(jnp.bfloat16), v_pair) / den)
    return jnp.where(low, outs[0], outs[1])


def _na_kernel(q_ref, k_ref, v_ref, bias_ref, o_ref):
    m = pl.program_id(1)
    rows = k_ref.shape[0] // GRID_W
    start_row = jnp.clip(m * NA_Q_ROWS - NA_WIN_H // 2, 0, rows - NA_K_ROWS)
    start = pl.multiple_of(start_row * GRID_W, GRID_W * NA_Q_ROWS)
    kwin = k_ref[pl.ds(start, NA_K), :]
    vwin = v_ref[pl.ds(start, NA_K), :]
    q = q_ref[...]
    for g in range(NA_HEADS // 2):
        sl = slice(g * LANES, (g + 1) * LANES)
        out = _pair_heads(q[:, sl], kwin[:, sl], vwin[:, sl], bias_ref[2 * g], bias_ref[2 * g + 1])
        o_ref[:, sl] = out.astype(o_ref.dtype)


def _na_case(m, nsteps):
    return jnp.where(m == 0, 0, jnp.where(m == nsteps - 1, 2, 1))


def _na_attn(q, k, v, bias):
    b, s, _ = q.shape
    nsteps = s // NA_Q
    return pl.pallas_call(
        _na_kernel,
        name="na_attn",
        grid=(b, nsteps),
        in_specs=[
            pl.BlockSpec((None, NA_Q, NA_DIM), lambda bi, m: (bi, m, 0)),
            pl.BlockSpec((None, s, NA_DIM), lambda bi, m: (bi, 0, 0)),
            pl.BlockSpec((None, s, NA_DIM), lambda bi, m: (bi, 0, 0)),
            pl.BlockSpec((None, NA_HEADS, NA_Q, NA_K), lambda bi, m: (_na_case(m, nsteps), 0, 0, 0)),
        ],
        out_specs=pl.BlockSpec((None, NA_Q, NA_DIM), lambda bi, m: (bi, m, 0)),
        out_shape=jax.ShapeDtypeStruct((b, s, NA_DIM), jnp.bfloat16),
        compiler_params=_params(("parallel", "arbitrary")),
    )(q, k, v, bias)


def _na_bias_tables(rpb, s):
    rows = s // GRID_W
    nsteps = rows // NA_Q_ROWS

    c = np.arange(GRID_W)[:, None]
    kc = np.arange(GRID_W)[None, :]
    c0 = np.clip(c - NA_WIN_W // 2, 0, GRID_W - NA_WIN_W)
    valid_c = (kc >= c0) & (kc < c0 + NA_WIN_W)
    sel_c = (kc - c + NA_WIN_W - 1)[..., None] == np.arange(2 * NA_WIN_W - 1)

    def row_geometry(m):
        start_row = int(np.clip(m * NA_Q_ROWS - NA_WIN_H // 2, 0, rows - NA_K_ROWS))
        r = (m * NA_Q_ROWS + np.arange(NA_Q_ROWS))[:, None]
        kr = (start_row + np.arange(NA_K_ROWS))[None, :]
        r0 = np.clip(r - NA_WIN_H // 2, 0, rows - NA_WIN_H)
        valid_r = (kr >= r0) & (kr < r0 + NA_WIN_H)
        sel_r = (kr - r + NA_WIN_H - 1)[..., None] == np.arange(2 * NA_WIN_H - 1)
        return valid_r, sel_r

    interior = row_geometry(1)
    for m in range(1, nsteps - 1):
        assert all(np.array_equal(a, b_) for a, b_ in zip(row_geometry(m), interior))
    tabs = []
    for valid_r, sel_r in (row_geometry(0), interior, row_geometry(nsteps - 1)):
        t = jnp.einsum("rka,abh,cjb->hrckj", sel_r.astype(np.float32), rpb.astype(jnp.float32),
                       sel_c.astype(np.float32), precision=lax.Precision.HIGHEST)
        valid = valid_r[:, None, :, None] & valid_c[None, :, None, :]
        tabs.append(jnp.where(valid[None], t, NEG_INF).reshape(NA_HEADS, NA_Q, NA_K))
    return jnp.stack(tabs)


def _swa_kernel(sink_ref, q_ref, k_ref, v_ref, bias_ref, o_ref):
    n = pl.program_id(1)
    s = k_ref.shape[0]
    start = pl.multiple_of(jnp.clip(n * SWA_Q - SWA_WINDOW, 0, s - SWA_K), SWA_WINDOW)
    kwin = k_ref[pl.ds(start, SWA_K), :]
    vwin = v_ref[pl.ds(start, SWA_K), :]
    q = q_ref[...]
    for g in range(SWA_GROUP):
        sl = slice(g * LANES, (g + 1) * LANES)
        out = _pair_heads(q[:, sl], kwin, vwin, bias_ref[2 * g], bias_ref[2 * g + 1],
                          sink_ref[g], sink_ref[g + SWA_GROUP])
        o_ref[:, sl] = out.astype(o_ref.dtype)


def _swa_attn(sink, q, k, v, bias):
    b, s, _ = q.shape
    nsteps = s // SWA_Q
    return pl.pallas_call(
        _swa_kernel,
        name="swa_attn",
        grid=(b, nsteps),
        in_specs=[
            pl.BlockSpec(memory_space=pltpu.SMEM),
            pl.BlockSpec((None, SWA_Q, SWA_DIM), lambda bi, n: (bi, n, 0)),
            pl.BlockSpec((None, s, SWA_KV_DIM), lambda bi, n: (bi, 0, 0)),
            pl.BlockSpec((None, s, SWA_KV_DIM), lambda bi, n: (bi, 0, 0)),
            pl.BlockSpec((None, SWA_HEADS, SWA_Q, SWA_K), lambda bi, n: (_na_case(n, nsteps), 0, 0, 0)),
        ],
        out_specs=pl.BlockSpec((None, SWA_Q, SWA_DIM), lambda bi, n: (bi, n, 0)),
        out_shape=jax.ShapeDtypeStruct((b, s, SWA_DIM), jnp.bfloat16),
        compiler_params=_params(("parallel", "arbitrary")),
    )(sink, q, k, v, bias)


def _t5_bucket(rel):
    nb = T5_BUCKETS // 2
    max_exact = nb // 2
    ret = (rel > 0).astype(np.int32) * nb
    n = np.abs(rel)
    large = max_exact + (np.log(np.maximum(n, 1) / max_exact) / np.log(T5_MAX_DIST / max_exact)
                         * (nb - max_exact)).astype(np.int32)
    large = np.minimum(large, nb - 1)
    return (ret + np.where(n < max_exact, n, large)).astype(np.int32)


_SWA_SLOT_HEADS = [h for g in range(SWA_GROUP) for h in (g, g + SWA_GROUP)]


def _swa_regroup(w, axis):
    blocks = [lax.slice_in_dim(w, h * HEAD_DIM, (h + 1) * HEAD_DIM, axis=axis) for h in _SWA_SLOT_HEADS]
    return jnp.concatenate(blocks, axis=axis)


def _swa_bias_tables(t5_bias, s):
    offsets = np.arange(-SWA_WINDOW, SWA_WINDOW + 1)
    sel = _t5_bucket(offsets)[:, None] == np.arange(T5_BUCKETS)
    bias_off = jnp.dot(sel.astype(np.float32), t5_bias.astype(jnp.float32), precision=lax.Precision.HIGHEST)
    bias_off = jnp.stack([bias_off[:, h] for h in _SWA_SLOT_HEADS])
    nsteps = s // SWA_Q
    period = SWA_Q + SWA_K - 1
    tabs = []
    for n in (0, 1, nsteps - 1):
        start = int(np.clip(n * SWA_Q - SWA_WINDOW, 0, s - SWA_K))
        delta = n * SWA_Q - start
        lead = SWA_Q - 1 + delta - SWA_WINDOW
        g = jnp.concatenate([jnp.full((SWA_HEADS, lead), NEG_INF, jnp.float32), bias_off,
                             jnp.full((SWA_HEADS, period - lead - 2 * SWA_WINDOW - 1), NEG_INF, jnp.float32)], axis=1)
        hankel = jnp.tile(g, (1, SWA_Q + 1))[:, :SWA_Q * (period + 1)].reshape(SWA_HEADS, SWA_Q, period + 1)
        tabs.append(jnp.flip(hankel[:, :, :SWA_K], axis=1))
    return jnp.stack(tabs)


def _mem_kv_kernel(mem_ref, g_ref, wk_ref, wv_ref, k_ref, v_ref):
    mn = _rms(mem_ref[...], g_ref[...]).astype(jnp.bfloat16)
    k_ref[...] = _dot(mn, wk_ref[...]).astype(jnp.bfloat16)
    v_ref[...] = _dot(mn, wv_ref[...]).astype(jnp.bfloat16)


def _mem_kv(mem, g, wk, wv):
    b, m, _ = mem.shape
    full = lambda shape: pl.BlockSpec(shape, lambda bi: (0,) * len(shape))
    return pl.pallas_call(
        _mem_kv_kernel,
        name="mem_kv",
        grid=(b,),
        in_specs=[pl.BlockSpec((None, m, D_MODEL), lambda bi: (bi, 0, 0)), full((1, D_MODEL)),
                  full((D_MODEL, XA_DIM)), full((D_MODEL, XA_DIM))],
        out_specs=[pl.BlockSpec((None, m, XA_DIM), lambda bi: (bi, 0, 0))] * 2,
        out_shape=[jax.ShapeDtypeStruct((b, m, XA_DIM), jnp.bfloat16)] * 2,
        compiler_params=_params(("parallel",)),
    )(mem, g, wk, wv)


def _mix_out_kernel(x_ref, yna_ref, ysw_ref, u_ref, uprev_ref, unext_ref, b_ref, cw_ref,
                    wna_ref, wsc_ref, wsw_ref, g_ref, wq_ref, km_ref, vm_ref, wo_ref, o_ref):
    i = pl.program_id(1)
    nt = pl.num_programs(1)
    u = u_ref[...]
    t = u.shape[0]
    row = lax.broadcasted_iota(jnp.int32, (t, 1), 0)
    prev_row = jnp.where(i > 0, uprev_ref[SUBLANES - 1:SUBLANES, :], 0.0)
    next_row = jnp.where(i < nt - 1, unext_ref[0:1, :], 0.0)
    u_m1 = jnp.where(row == 0, prev_row, pltpu.roll(u, 1, axis=0))
    u_p1 = jnp.where(row == t - 1, next_row, pltpu.roll(u, t - 1, axis=0))
    ysc = b_ref[...] * (u_m1 * cw_ref[0:1, :] + u * cw_ref[1:2, :] + u_p1 * cw_ref[2:3, :])
    y = _dot(yna_ref[...], wna_ref[...]) + _dot(ysc.astype(jnp.bfloat16), wsc_ref[...]) \
        + _dot(ysw_ref[...], wsw_ref[...])
    x1 = x_ref[...] + y

    xn = _rms(x1, g_ref[...]).astype(jnp.bfloat16)
    q = _dot(xn, wq_ref[...])
    outs = []
    for h in range(XA_HEADS):
        sl = slice(h * XA_HEAD_DIM, (h + 1) * XA_HEAD_DIM)
        logits = _dot_nt(q[:, sl].astype(jnp.bfloat16), km_ref[:, sl]) * (XA_HEAD_DIM ** -0.5)
        mx = jnp.max(logits, axis=-1, keepdims=True)
        p = jnp.exp(logits - mx)
        den = jnp.sum(p, axis=-1, keepdims=True)
        outs.append((_dot(p.astype(jnp.bfloat16), vm_ref[:, sl]) / den).astype(jnp.bfloat16))
    o = jnp.concatenate(outs, axis=-1)
    o_ref[...] = x1 + _dot(o, wo_ref[...])


def _mix_out(x, yna, ysw, u, bgate, cw, wna, wsc, wsw, g, wq, km, vm, wo):
    b, s, _ = x.shape
    nt = s // TOK_TILE
    halo_blocks = TOK_TILE // SUBLANES
    last_halo = s // SUBLANES - 1
    tok = lambda d: pl.BlockSpec((None, TOK_TILE, d), lambda bi, i: (bi, i, 0))
    full = lambda shape: pl.BlockSpec(shape, lambda bi, i: (0,) * len(shape))
    mem = pl.BlockSpec((None, km.shape[1], XA_DIM), lambda bi, i: (bi, 0, 0))
    return pl.pallas_call(
        _mix_out_kernel,
        name="mix_out",
        grid=(b, nt),
        in_specs=[
            tok(D_MODEL), tok(NA_DIM), tok(SWA_DIM), tok(SC_DIM),
            pl.BlockSpec((None, SUBLANES, SC_DIM), lambda bi, i: (bi, jnp.maximum(i * halo_blocks - 1, 0), 0)),
            pl.BlockSpec((None, SUBLANES, SC_DIM), lambda bi, i: (bi, jnp.minimum((i + 1) * halo_blocks, last_halo), 0)),
            tok(SC_DIM), full((3, SC_DIM)),
            full((NA_DIM, D_MODEL)), full((SC_DIM, D_MODEL)), full((SWA_DIM, D_MODEL)),
            full((1, D_MODEL)), full((D_MODEL, XA_DIM)), mem, mem, full((XA_DIM, D_MODEL)),
        ],
        out_specs=tok(D_MODEL),
        out_shape=jax.ShapeDtypeStruct((b, s, D_MODEL), jnp.float32),
        compiler_params=_params(("parallel", "parallel")),
    )(x, yna, ysw, u, u, u, bgate, cw, wna, wsc, wsw, g, wq, km, vm, wo)


_CAND_PAIRS = [(p, q) for p in range(PEER_TOPK) for q in range(PEER_TOPK) if (p + 1) * (q + 1) <= PEER_TOPK]
_CAND_VREGS = -(-len(_CAND_PAIRS) // SUBLANES)


def _allmax_sublanes(v):
    for shift in (4, 2, 1):
        v = jnp.maximum(v, pltpu.roll(v, shift, axis=0))
    return v


def _top_values(s3, count):
    vals = []
    for r in range(count):
        m = _allmax_sublanes(jnp.max(s3, axis=0))
        vals.append(m)
        if r + 1 < count:
            s3 = jnp.where(s3 == m[None], -jnp.inf, s3)
    return vals


def _batcher_pairs(n):
    pairs = []
    p = 1
    while p < n:
        k = p
        while k >= 1:
            for j in range(k % p, n - k, 2 * k):
                for i in range(min(k, n - j - k)):
                    if (i + j) // (2 * p) == (i + j + k) // (2 * p):
                        pairs.append((i + j, i + j + k))
            k //= 2
        p *= 2
    return pairs


_SORT16 = _batcher_pairs(PEER_TOPK)
_BITONIC16 = [(i, i + k) for k in (8, 4, 2, 1) for i in range(PEER_TOPK) if not i & k]


def _descending(vals, pairs):
    vals = list(vals)
    for i, j in pairs:
        vals[i], vals[j] = jnp.maximum(vals[i], vals[j]), jnp.minimum(vals[i], vals[j])
    return vals


def _top16_sorted(s3):
    vals = _descending([s3[k] for k in range(PEER_TOPK)], _SORT16)
    for shift in (4, 2, 1):
        other = [pltpu.roll(v, shift, axis=0) for v in vals]
        vals = _descending([jnp.maximum(vals[k], other[PEER_TOPK - 1 - k]) for k in range(PEER_TOPK)], _BITONIC16)
    return vals


def _peer_prep_chunk(s0, s1):
    groups = PEER_NKEYS // SUBLANES
    a = _top16_sorted(s0.reshape(groups, SUBLANES, LANES))
    b = _top16_sorted(s1.reshape(groups, SUBLANES, LANES))
    sub = lax.broadcasted_iota(jnp.int32, (SUBLANES, LANES), 0)
    cand = []
    for v in range(_CAND_VREGS):
        acc = jnp.full((SUBLANES, LANES), -jnp.inf, jnp.float32)
        for k, (p, q) in enumerate(_CAND_PAIRS[v * SUBLANES:(v + 1) * SUBLANES]):
            acc = jnp.where(sub == k, a[p] + b[q], acc)
        cand.append(acc)
    best = _top_values(jnp.stack(cand), PEER_TOPK)
    z = jnp.ones((SUBLANES, LANES), jnp.float32)
    for r in range(1, PEER_TOPK):
        z = z + jnp.exp(best[r] - best[0])
    tau = best[PEER_TOPK - 1][0:1, :]
    floor = jnp.exp((tau - b[0][0:1, :]) - s0)
    rowgate = jnp.exp(s0 - a[0][0:1, :]) * (0.5 / z[0:1, :])
    colgate = jnp.exp(s1 - b[0][0:1, :])
    return floor, rowgate, colgate


def _row(words, k):
    return jnp.broadcast_to(words[k:k + 1, :], (PEER_NKEYS, LANES))


def _peer_gate_rows(eb, rows, floor_ref, rowg_ref, colg_ref, tile_ref, gt_ref):
    base = rows[0] - rows[0] % SUBLANES
    assert all(base <= ii < base + SUBLANES for ii in rows)
    i0 = pl.multiple_of(eb * (PEER_EB // PEER_NKEYS) + base, SUBLANES)
    for c in range(PEER_CHUNKS):
        w = [jnp.zeros((PEER_NKEYS, LANES), jnp.float32) for _ in rows]
        for h in range(PEER_HEADS):
            colgate = colg_ref[h, c]
            floors = floor_ref[h, c, pl.ds(i0, SUBLANES), :]
            rowgates = rowg_ref[h, c, pl.ds(i0, SUBLANES), :]
            for n, ii in enumerate(rows):
                w[n] = w[n] + jnp.where(colgate >= _row(floors, ii - base), colgate, 0.0) * _row(rowgates, ii - base)
        for n, ii in enumerate(rows):
            gt_ref[ii * PEER_NKEYS:(ii + 1) * PEER_NKEYS, c * LANES:(c + 1) * LANES] = (w[n] * tile_ref[ii, c]).astype(jnp.bfloat16)


def _peer_kernel(x_ref, g_ref, wqt_ref, sk_ref, u_ref, vt_ref, fg_ref, o_ref,
                 xnt_ref, qt_ref, tile_ref, floor_ref, rowg_ref, colg_ref, gt_ref, acc_ref, *, final_norm):
    eb = pl.program_id(2)
    neb = pl.num_programs(2)

    @pl.when(eb == 0)
    def _prep():
        xn = _rms(x_ref[...], g_ref[...])
        xnt_ref[...] = xn.T.astype(jnp.bfloat16)
        acc_ref[...] = jnp.zeros_like(acc_ref)

        q_rows = 2 * PEER_HEADS * PEER_HALF // PEER_Q_PIECES
        for k in range(PEER_Q_PIECES):
            rows = slice(k * q_rows, (k + 1) * q_rows)
            qt_ref[rows, :] = _dot(wqt_ref[rows, :], xnt_ref[...]).astype(jnp.bfloat16)

        for hp in range(2 * PEER_HEADS):
            s = _dot(sk_ref[hp], qt_ref[hp * PEER_HALF:(hp + 1) * PEER_HALF, :])
            for c in range(PEER_CHUNKS):
                tile_ref[hp, c] = s[:, c * LANES:(c + 1) * LANES]

        def chunk_pair(idx, carry):
            h = idx // (PEER_CHUNKS // 2)
            for c in (2 * (idx % (PEER_CHUNKS // 2)), 2 * (idx % (PEER_CHUNKS // 2)) + 1):
                floor_ref[h, c], rowg_ref[h, c], colg_ref[h, c] = _peer_prep_chunk(
                    tile_ref[2 * h, c], tile_ref[2 * h + 1, c])
            return carry

        lax.fori_loop(0, PEER_HEADS * PEER_CHUNKS // 2, chunk_pair, 0)

    for ii in range(PEER_EB // PEER_NKEYS):
        hidden = _dot(u_ref[ii * PEER_NKEYS:(ii + 1) * PEER_NKEYS, :], xnt_ref[...])
        act = hidden + hidden * lax.erf(hidden * (2.0 ** -0.5))
        for c in range(PEER_CHUNKS):
            tile_ref[ii, c] = act[:, c * LANES:(c + 1) * LANES]

    for r0 in range(0, PEER_EB // PEER_NKEYS, PEER_GATE_ROWS):
        _peer_gate_rows(eb, list(range(r0, r0 + PEER_GATE_ROWS)), floor_ref, rowg_ref, colg_ref, tile_ref, gt_ref)
    acc_ref[...] += _dot(vt_ref[...], gt_ref[...])

    @pl.when(eb == neb - 1)
    def _finish():
        y = x_ref[...] + acc_ref[...].T
        if final_norm:
            y = _rms(y, fg_ref[...])
        o_ref[...] = y


def _peer(x, g, wqt, sk, u, vt, fg, final_norm):
    b, s, _ = x.shape
    tok = pl.BlockSpec((None, PEER_TOK, D_MODEL), lambda bi, i, e: (bi, i, 0))
    full = lambda shape: pl.BlockSpec(shape, lambda bi, i, e: (0,) * len(shape), pipeline_mode=pl.Buffered(1))
    gate_scratch = pltpu.VMEM((PEER_HEADS, PEER_CHUNKS, PEER_NKEYS, LANES), jnp.float32)
    return pl.pallas_call(
        functools.partial(_peer_kernel, final_norm=final_norm),
        name="peer",
        grid=(b, s // PEER_TOK, PEER_EXPERTS // PEER_EB),
        in_specs=[
            tok, full((1, D_MODEL)), full((2 * PEER_HEADS * PEER_HALF, D_MODEL)),
            full((2 * PEER_HEADS, PEER_NKEYS, PEER_HALF)),
            pl.BlockSpec((PEER_EB, D_MODEL), lambda bi, i, e: (e, 0)),
            pl.BlockSpec((D_MODEL, PEER_EB), lambda bi, i, e: (0, e)),
            full((1, D_MODEL)),
        ],
        out_specs=tok,
        out_shape=jax.ShapeDtypeStruct((b, s, D_MODEL), jnp.float32),
        scratch_shapes=[
            pltpu.VMEM((D_MODEL, PEER_TOK), jnp.bfloat16),
            pltpu.VMEM((2 * PEER_HEADS * PEER_HALF, PEER_TOK), jnp.bfloat16),
            pltpu.VMEM((max(2 * PEER_HEADS, PEER_EB // PEER_NKEYS), PEER_CHUNKS, PEER_NKEYS, LANES), jnp.float32),
            gate_scratch, gate_scratch, gate_scratch,
            pltpu.VMEM((PEER_EB, PEER_TOK), jnp.bfloat16),
            pltpu.VMEM((D_MODEL, PEER_TOK), jnp.float32),
        ],
        compiler_params=_params(("parallel", "parallel", "arbitrary")),
    )(x, g, wqt, sk, u, vt, fg)


def _prepare_layer(l, w_in, conv_w, swa_sink, w_out, norm_mix_g, norm_xa_g, norm_mem_g, w_xq, w_xk, w_xv, w_xo,
                   norm_ffn_g, peer_wq, peer_subkeys, peer_u, peer_v):
    bf = jnp.bfloat16
    scale = HEAD_DIM ** -0.5
    o = np.cumsum([0, NA_DIM, NA_DIM, NA_DIM, SC_DIM, SC_DIM, SC_DIM, SWA_DIM, SWA_KV_DIM, SWA_KV_DIM])
    wi = w_in[l]
    w_in_l = jnp.concatenate([
        wi[:, o[0]:o[1]] * scale, wi[:, o[1]:o[6]], _swa_regroup(wi[:, o[6]:o[7]] * scale, 1), wi[:, o[7]:o[9]],
    ], axis=1).astype(bf)
    wo = w_out[l]
    row = lambda v: v.reshape(1, -1).astype(jnp.float32)
    return dict(
        g_mix=row(norm_mix_g[l]), w_in=w_in_l, conv_w=conv_w[l].astype(jnp.float32),
        sink=swa_sink[l].astype(jnp.float32),
        wna=wo[:NA_DIM].astype(bf), wsc=wo[NA_DIM:NA_DIM + SC_DIM].astype(bf),
        wsw=_swa_regroup(wo[NA_DIM + SC_DIM:], 0).astype(bf),
        g_xa=row(norm_xa_g[l]), g_mem=row(norm_mem_g[l]),
        wxq=w_xq[l].astype(bf), wxk=w_xk[l].astype(bf), wxv=w_xv[l].astype(bf), wxo=w_xo[l].astype(bf),
        g_ffn=row(norm_ffn_g[l]), wqt=peer_wq[l].T.astype(bf),
        sk=peer_subkeys[l].reshape(2 * PEER_HEADS, PEER_NKEYS, PEER_HALF).astype(bf),
        u=peer_u[l].astype(bf), vt=peer_v[l].T.astype(bf),
    )


def _trunk(x, mem, layers, na_bias, swa_bias, final_g):
    depth = len(layers)
    for l, p in enumerate(layers):
        naq, nak, nav, u, bgate, swq, swk, swv = _mix_in(x, p["g_mix"], p["w_in"])
        yna = _na_attn(naq, nak, nav, na_bias[l])
        ysw = _swa_attn(p["sink"], swq, swk, swv, swa_bias)
        km, vm = _mem_kv(mem, p["g_mem"], p["wxk"], p["wxv"])
        x = _mix_out(x, yna, ysw, u, bgate, p["conv_w"], p["wna"], p["wsc"], p["wsw"],
                     p["g_xa"], p["wxq"], km, vm, p["wxo"])
        x = _peer(x, p["g_ffn"], p["wqt"], p["sk"], p["u"], p["vt"], final_g, final_norm=(l == depth - 1))
    return x


def kernel(x_prompt, x_sample, mem_prompt, mem_sample, norm_mix_g, w_in, na_rpb, conv_w, swa_sink, t5_bias, w_out,
           norm_xa_g, norm_mem_g, w_xq, w_xk, w_xv, w_xo, norm_ffn_g, peer_wq, peer_subkeys, peer_u, peer_v, final_g):
    depth = w_in.shape[0]
    layers = [_prepare_layer(l, w_in, conv_w, swa_sink, w_out, norm_mix_g, norm_xa_g, norm_mem_g, w_xq, w_xk, w_xv,
                             w_xo, norm_ffn_g, peer_wq, peer_subkeys, peer_u, peer_v) for l in range(depth)]
    fg = final_g.reshape(1, -1).astype(jnp.float32)
    tables = {}
    outs = []
    for x, mem in ((x_prompt, mem_prompt), (x_sample, mem_sample)):
        s = x.shape[1]
        if s not in tables:
            tables[s] = ([_na_bias_tables(na_rpb[l], s) for l in range(depth)], _swa_bias_tables(t5_bias, s))
        outs.append(_trunk(x, mem, layers, *tables[s], fg))
    return tuple(outs)
```

```python
import functools

import numpy as np
import jax
import jax.numpy as jnp
from jax import lax
from jax.experimental import pallas as pl
from jax.experimental.pallas import tpu as pltpu

D_MODEL = 1024
GRID_W = 64
HEAD_DIM = 64
NA_HEADS = 6
NA_WIN_H = 8
NA_WIN_W = 16
NA_DIM = NA_HEADS * HEAD_DIM
SC_DIM = 256
SWA_HEADS = 6
SWA_KV_HEADS = 2
SWA_GROUP = SWA_HEADS // SWA_KV_HEADS
SWA_WINDOW = 128
SWA_DIM = SWA_HEADS * HEAD_DIM
SWA_KV_DIM = SWA_KV_HEADS * HEAD_DIM
T5_BUCKETS = 32
T5_MAX_DIST = 128
IN_DIM = 3 * NA_DIM + 3 * SC_DIM + SWA_DIM + 2 * SWA_KV_DIM
XA_HEADS = 4
XA_HEAD_DIM = 128
XA_DIM = XA_HEADS * XA_HEAD_DIM
PEER_HEADS = 8
PEER_NKEYS = 128
PEER_EXPERTS = PEER_NKEYS * PEER_NKEYS
PEER_TOPK = 16
PEER_HALF = 128
RMS_EPS = 1e-6
NEG_INF = -1e30

LANES = 128
SUBLANES = 8
VMEM_LIMIT = 56 * 1024 * 1024

TOK_TILE = 512
NA_Q_ROWS = 4
NA_K_ROWS = NA_Q_ROWS + NA_WIN_H
NA_Q = NA_Q_ROWS * GRID_W
NA_K = NA_K_ROWS * GRID_W
SWA_Q = 256
SWA_K = SWA_Q + 2 * SWA_WINDOW
PEER_TOK = 512
PEER_EB = 2048
PEER_CHUNKS = PEER_TOK // LANES
PEER_GATE_ROWS = 2
PEER_Q_PIECES = 4

_NT = (((1,), (1,)), ((), ()))


def _rms(xf, g):
    return xf * lax.rsqrt(jnp.mean(xf * xf, axis=-1, keepdims=True) + RMS_EPS) * g


def _dot(a, b):
    return jnp.dot(a, b, preferred_element_type=jnp.float32)


def _dot_nt(a, b):
    return lax.dot_general(a, b, _NT, preferred_element_type=jnp.float32)


def _params(sem):
    return pltpu.CompilerParams(dimension_semantics=sem, vmem_limit_bytes=VMEM_LIMIT)


def _mix_in_kernel(x_ref, g_ref, w_ref, naq_ref, nak_ref, nav_ref, u_ref, b_ref, swq_ref, swk_ref, swv_ref):
    xn = _rms(x_ref[...], g_ref[...]).astype(jnp.bfloat16)
    z = _dot(xn, w_ref[...])
    o = 0
    naq_ref[...] = z[:, o:o + NA_DIM].astype(jnp.bfloat16); o += NA_DIM
    nak_ref[...] = z[:, o:o + NA_DIM].astype(jnp.bfloat16); o += NA_DIM
    nav_ref[...] = z[:, o:o + NA_DIM].astype(jnp.bfloat16); o += NA_DIM
    b_ref[...] = z[:, o:o + SC_DIM]; o += SC_DIM
    u_ref[...] = z[:, o:o + SC_DIM] * z[:, o + SC_DIM:o + 2 * SC_DIM]; o += 2 * SC_DIM
    swq_ref[...] = z[:, o:o + SWA_DIM].astype(jnp.bfloat16); o += SWA_DIM
    swk_ref[...] = z[:, o:o + SWA_KV_DIM].astype(jnp.bfloat16); o += SWA_KV_DIM
    swv_ref[...] = z[:, o:o + SWA_KV_DIM].astype(jnp.bfloat16)


def _mix_in(x, g, w):
    b, s, _ = x.shape
    tok = lambda d: pl.BlockSpec((None, TOK_TILE, d), lambda bi, i: (bi, i, 0))
    full = lambda shape: pl.BlockSpec(shape, lambda bi, i: (0,) * len(shape))
    widths = [(NA_DIM, jnp.bfloat16)] * 3 + [(SC_DIM, jnp.float32)] * 2 + \
             [(SWA_DIM, jnp.bfloat16), (SWA_KV_DIM, jnp.bfloat16), (SWA_KV_DIM, jnp.bfloat16)]
    return pl.pallas_call(
        _mix_in_kernel,
        name="mix_in",
        grid=(b, s // TOK_TILE),
        in_specs=[tok(D_MODEL), full((1, D_MODEL)), full((D_MODEL, IN_DIM))],
        out_specs=[tok(d) for d, _ in widths],
        out_shape=[jax.ShapeDtypeStruct((b, s, d), t) for d, t in widths],
        compiler_params=_params(("parallel", "parallel")),
    )(x, g, w)


def _pair_heads(q_pair, k_pair, v_pair, bias_lo, bias_hi, extra_lo=None, extra_hi=None):
    low = lax.broadcasted_iota(jnp.int32, (1, LANES), 1) < HEAD_DIM
    outs = []
    for keep, bias, extra in ((low, bias_lo, extra_lo), (~low, bias_hi, extra_hi)):
        qm = jnp.where(keep, q_pair, jnp.zeros_like(q_pair))
        logits = _dot_nt(qm, k_pair) + bias
        mx = jnp.max(logits, axis=-1, keepdims=True)
        if extra is not None:
            mx = jnp.maximum(mx, extra)
        p = jnp.exp(logits - mx)
        den = jnp.sum(p, axis=-1, keepdims=True)
        if extra is not None:
            den = den + jnp.exp(extra - mx)
        outs.append(_dot(p.astype(jnp.bfloat16), v_pair) / den)
    return jnp.where(low, outs[0], outs[1])


def _na_kernel(q_ref, k_ref, v_ref, bias_ref, o_ref):
    m = pl.program_id(1)
    rows = k_ref.shape[0] // GRID_W
    start_row = jnp.clip(m * NA_Q_ROWS - NA_WIN_H // 2, 0, rows - NA_K_ROWS)
    start = pl.multiple_of(start_row * GRID_W, GRID_W * NA_Q_ROWS)
    kwin = k_ref[pl.ds(start, NA_K), :]
    vwin = v_ref[pl.ds(start, NA_K), :]
    q = q_ref[...]
    for g in range(NA_HEADS // 2):
        sl = slice(g * LANES, (g + 1) * LANES)
        out = _pair_heads(q[:, sl], kwin[:, sl], vwin[:, sl], bias_ref[2 * g], bias_ref[2 * g + 1])
        o_ref[:, sl] = out.astype(o_ref.dtype)


def _na_case(m, nsteps):
    return jnp.where(m == 0, 0, jnp.where(m == nsteps - 1, 2, 1))


def _na_attn(q, k, v, bias):
    b, s, _ = q.shape
    nsteps = s // NA_Q
    return pl.pallas_call(
        _na_kernel,
        name="na_attn",
        grid=(b, nsteps),
        in_specs=[
            pl.BlockSpec((None, NA_Q, NA_DIM), lambda bi, m: (bi, m, 0)),
            pl.BlockSpec((None, s, NA_DIM), lambda bi, m: (bi, 0, 0)),
            pl.BlockSpec((None, s, NA_DIM), lambda bi, m: (bi, 0, 0)),
            pl.BlockSpec((None, NA_HEADS, NA_Q, NA_K), lambda bi, m: (_na_case(m, nsteps), 0, 0, 0)),
        ],
        out_specs=pl.BlockSpec((None, NA_Q, NA_DIM), lambda bi, m: (bi, m, 0)),
        out_shape=jax.ShapeDtypeStruct((b, s, NA_DIM), jnp.bfloat16),
        compiler_params=_params(("parallel", "arbitrary")),
    )(q, k, v, bias)


def _na_bias_tables(rpb, s):
    rows = s // GRID_W
    nsteps = rows // NA_Q_ROWS

    c = np.arange(GRID_W)[:, None]
    kc = np.arange(GRID_W)[None, :]
    c0 = np.clip(c - NA_WIN_W // 2, 0, GRID_W - NA_WIN_W)
    valid_c = (kc >= c0) & (kc < c0 + NA_WIN_W)
    sel_c = (kc - c + NA_WIN_W - 1)[..., None] == np.arange(2 * NA_WIN_W - 1)

    def row_geometry(m):
        start_row = int(np.clip(m * NA_Q_ROWS - NA_WIN_H // 2, 0, rows - NA_K_ROWS))
        r = (m * NA_Q_ROWS + np.arange(NA_Q_ROWS))[:, None]
        kr = (start_row + np.arange(NA_K_ROWS))[None, :]
        r0 = np.clip(r - NA_WIN_H // 2, 0, rows - NA_WIN_H)
        valid_r = (kr >= r0) & (kr < r0 + NA_WIN_H)
        sel_r = (kr - r + NA_WIN_H - 1)[..., None] == np.arange(2 * NA_WIN_H - 1)
        return valid_r, sel_r

    interior = row_geometry(1)
    for m in range(1, nsteps - 1):
        assert all(np.array_equal(a, b_) for a, b_ in zip(row_geometry(m), interior))
    tabs = []
    for valid_r, sel_r in (row_geometry(0), interior, row_geometry(nsteps - 1)):
        t = jnp.einsum("rka,abh,cjb->hrckj", sel_r.astype(np.float32), rpb.astype(jnp.float32),
                       sel_c.astype(np.float32), precision=lax.Precision.HIGHEST)
        valid = valid_r[:, None, :, None] & valid_c[None, :, None, :]
        tabs.append(jnp.where(valid[None], t, NEG_INF).reshape(NA_HEADS, NA_Q, NA_K))
    return jnp.stack(tabs)


def _swa_kernel(sink_ref, q_ref, k_ref, v_ref, bias_ref, o_ref):
    n = pl.program_id(1)
    s = k_ref.shape[0]
    start = pl.multiple_of(jnp.clip(n * SWA_Q - SWA_WINDOW, 0, s - SWA_K), SWA_WINDOW)
    kwin = k_ref[pl.ds(start, SWA_K), :]
    vwin = v_ref[pl.ds(start, SWA_K), :]
    q = q_ref[...]
    for g in range(SWA_GROUP):
        sl = slice(g * LANES, (g + 1) * LANES)
        out = _pair_heads(q[:, sl], kwin, vwin, bias_ref[2 * g], bias_ref[2 * g + 1],
                          sink_ref[g], sink_ref[g + SWA_GROUP])
        o_ref[:, sl] = out.astype(o_ref.dtype)


def _swa_attn(sink, q, k, v, bias):
    b, s, _ = q.shape
    nsteps = s // SWA_Q
    return pl.pallas_call(
        _swa_kernel,
        name="swa_attn",
        grid=(b, nsteps),
        in_specs=[
            pl.BlockSpec(memory_space=pltpu.SMEM),
            pl.BlockSpec((None, SWA_Q, SWA_DIM), lambda bi, n: (bi, n, 0)),
            pl.BlockSpec((None, s, SWA_KV_DIM), lambda bi, n: (bi, 0, 0)),
            pl.BlockSpec((None, s, SWA_KV_DIM), lambda bi, n: (bi, 0, 0)),
            pl.BlockSpec((None, SWA_HEADS, SWA_Q, SWA_K), lambda bi, n: (_na_case(n, nsteps), 0, 0, 0)),
        ],
        out_specs=pl.BlockSpec((None, SWA_Q, SWA_DIM), lambda bi, n: (bi, n, 0)),
        out_shape=jax.ShapeDtypeStruct((b, s, SWA_DIM), jnp.bfloat16),
        compiler_params=_params(("parallel", "arbitrary")),
    )(sink, q, k, v, bias)


def _t5_bucket(rel):
    nb = T5_BUCKETS // 2
    max_exact = nb // 2
    ret = (rel > 0).astype(np.int32) * nb
    n = np.abs(rel)
    large = max_exact + (np.log(np.maximum(n, 1) / max_exact) / np.log(T5_MAX_DIST / max_exact)
                         * (nb - max_exact)).astype(np.int32)
    large = np.minimum(large, nb - 1)
    return (ret + np.where(n < max_exact, n, large)).astype(np.int32)


_SWA_SLOT_HEADS = [h for g in range(SWA_GROUP) for h in (g, g + SWA_GROUP)]


def _swa_regroup(w, axis):
    blocks = [lax.slice_in_dim(w, h * HEAD_DIM, (h + 1) * HEAD_DIM, axis=axis) for h in _SWA_SLOT_HEADS]
    return jnp.concatenate(blocks, axis=axis)


def _swa_bias_tables(t5_bias, s):
    offsets = np.arange(-SWA_WINDOW, SWA_WINDOW + 1)
    sel = _t5_bucket(offsets)[:, None] == np.arange(T5_BUCKETS)
    bias_off = jnp.dot(sel.astype(np.float32), t5_bias.astype(jnp.float32), precision=lax.Precision.HIGHEST)
    bias_off = jnp.stack([bias_off[:, h] for h in _SWA_SLOT_HEADS])
    nsteps = s // SWA_Q
    period = SWA_Q + SWA_K - 1
    tabs = []
    for n in (0, 1, nsteps - 1):
        start = int(np.clip(n * SWA_Q - SWA_WINDOW, 0, s - SWA_K))
        delta = n * SWA_Q - start
        lead = SWA_Q - 1 + delta - SWA_WINDOW
        g = jnp.concatenate([jnp.full((SWA_HEADS, lead), NEG_INF, jnp.float32), bias_off,
                             jnp.full((SWA_HEADS, period - lead - 2 * SWA_WINDOW - 1), NEG_INF, jnp.float32)], axis=1)
        hankel = jnp.tile(g, (1, SWA_Q + 1))[:, :SWA_Q * (period + 1)].reshape(SWA_HEADS, SWA_Q, period + 1)
        tabs.append(jnp.flip(hankel[:, :, :SWA_K], axis=1))
    return jnp.stack(tabs)


def _mem_kv_kernel(mem_ref, g_ref, wk_ref, wv_ref, k_ref, v_ref):
    mn = _rms(mem_ref[...], g_ref[...]).astype(jnp.bfloat16)
    k_ref[...] = _dot(mn, wk_ref[...]).astype(jnp.bfloat16)
    v_ref[...] = _dot(mn, wv_ref[...]).astype(jnp.bfloat16)


def _mem_kv(mem, g, wk, wv):
    b, m, _ = mem.shape
    full = lambda shape: pl.BlockSpec(shape, lambda bi: (0,) * len(shape))
    return pl.pallas_call(
        _mem_kv_kernel,
        name="mem_kv",
        grid=(b,),
        in_specs=[pl.BlockSpec((None, m, D_MODEL), lambda bi: (bi, 0, 0)), full((1, D_MODEL)),
                  full((D_MODEL, XA_DIM)), full((D_MODEL, XA_DIM))],
        out_specs=[pl.BlockSpec((None, m, XA_DIM), lambda bi: (bi, 0, 0))] * 2,
        out_shape=[jax.ShapeDtypeStruct((b, m, XA_DIM), jnp.bfloat16)] * 2,
        compiler_params=_params(("parallel",)),
    )(mem, g, wk, wv)


def _mix_out_kernel(x_ref, yna_ref, ysw_ref, u_ref, uprev_ref, unext_ref, b_ref, cw_ref,
                    wna_ref, wsc_ref, wsw_ref, g_ref, wq_ref, km_ref, vm_ref, wo_ref, o_ref):
    i = pl.program_id(1)
    nt = pl.num_programs(1)
    u = u_ref[...]
    t = u.shape[0]
    row = lax.broadcasted_iota(jnp.int32, (t, 1), 0)
    prev_row = jnp.where(i > 0, uprev_ref[SUBLANES - 1:SUBLANES, :], 0.0)
    next_row = jnp.where(i < nt - 1, unext_ref[0:1, :], 0.0)
    u_m1 = jnp.where(row == 0, prev_row, pltpu.roll(u, 1, axis=0))
    u_p1 = jnp.where(row == t - 1, next_row, pltpu.roll(u, t - 1, axis=0))
    ysc = b_ref[...] * (u_m1 * cw_ref[0:1, :] + u * cw_ref[1:2, :] + u_p1 * cw_ref[2:3, :])
    y = _dot(yna_ref[...], wna_ref[...]) + _dot(ysc.astype(jnp.bfloat16), wsc_ref[...]) \
        + _dot(ysw_ref[...], wsw_ref[...])
    x1 = x_ref[...] + y

    xn = _rms(x1, g_ref[...]).astype(jnp.bfloat16)
    q = _dot(xn, wq_ref[...])
    outs = []
    for h in range(XA_HEADS):
        sl = slice(h * XA_HEAD_DIM, (h + 1) * XA_HEAD_DIM)
        logits = _dot_nt(q[:, sl].astype(jnp.bfloat16), km_ref[:, sl]) * (XA_HEAD_DIM ** -0.5)
        mx = jnp.max(logits, axis=-1, keepdims=True)
        p = jnp.exp(logits - mx)
        den = jnp.sum(p, axis=-1, keepdims=True)
        outs.append((_dot(p.astype(jnp.bfloat16), vm_ref[:, sl]) / den).astype(jnp.bfloat16))
    o = jnp.concatenate(outs, axis=-1)
    o_ref[...] = x1 + _dot(o, wo_ref[...])


def _mix_out(x, yna, ysw, u, bgate, cw, wna, wsc, wsw, g, wq, km, vm, wo):
    b, s, _ = x.shape
    nt = s // TOK_TILE
    halo_blocks = TOK_TILE // SUBLANES
    last_halo = s // SUBLANES - 1
    tok = lambda d: pl.BlockSpec((None, TOK_TILE, d), lambda bi, i: (bi, i, 0))
    full = lambda shape: pl.BlockSpec(shape, lambda bi, i: (0,) * len(shape))
    mem = pl.BlockSpec((None, km.shape[1], XA_DIM), lambda bi, i: (bi, 0, 0))
    return pl.pallas_call(
        _mix_out_kernel,
        name="mix_out",
        grid=(b, nt),
        in_specs=[
            tok(D_MODEL), tok(NA_DIM), tok(SWA_DIM), tok(SC_DIM),
            pl.BlockSpec((None, SUBLANES, SC_DIM), lambda bi, i: (bi, jnp.maximum(i * halo_blocks - 1, 0), 0)),
            pl.BlockSpec((None, SUBLANES, SC_DIM), lambda bi, i: (bi, jnp.minimum((i + 1) * halo_blocks, last_halo), 0)),
            tok(SC_DIM), full((3, SC_DIM)),
            full((NA_DIM, D_MODEL)), full((SC_DIM, D_MODEL)), full((SWA_DIM, D_MODEL)),
            full((1, D_MODEL)), full((D_MODEL, XA_DIM)), mem, mem, full((XA_DIM, D_MODEL)),
        ],
        out_specs=tok(D_MODEL),
        out_shape=jax.ShapeDtypeStruct((b, s, D_MODEL), jnp.float32),
        compiler_params=_params(("parallel", "parallel")),
    )(x, yna, ysw, u, u, u, bgate, cw, wna, wsc, wsw, g, wq, km, vm, wo)


_CAND_PAIRS = [(p, q) for p in range(PEER_TOPK) for q in range(PEER_TOPK) if (p + 1) * (q + 1) <= PEER_TOPK]
_CAND_VREGS = -(-len(_CAND_PAIRS) // SUBLANES)


def _allmax_sublanes(v):
    for shift in (4, 2, 1):
        v = jnp.maximum(v, pltpu.roll(v, shift, axis=0))
    return v


def _top_values(s3, count):
    vals = []
    for r in range(count):
        m = _allmax_sublanes(jnp.max(s3, axis=0))
        vals.append(m)
        if r + 1 < count:
            s3 = jnp.where(s3 == m[None], -jnp.inf, s3)
    return vals


def _batcher_pairs(n):
    pairs = []
    p = 1
    while p < n:
        k = p
        while k >= 1:
            for j in range(k % p, n - k, 2 * k):
                for i in range(min(k, n - j - k)):
                    if (i + j) // (2 * p) == (i + j + k) // (2 * p):
                        pairs.append((i + j, i + j + k))
            k //= 2
        p *= 2
    return pairs


_TIE_SLACK = 1.0 - 2.0 ** -18
_SORT16 = _batcher_pairs(PEER_TOPK)
_BITONIC16 = [(i, i + k) for k in (8, 4, 2, 1) for i in range(PEER_TOPK) if not i & k]


def _descending(vals, pairs):
    vals = list(vals)
    for i, j in pairs:
        vals[i], vals[j] = jnp.maximum(vals[i], vals[j]), jnp.minimum(vals[i], vals[j])
    return vals


def _top16_sorted(s3):
    vals = _descending([s3[k] for k in range(PEER_TOPK)], _SORT16)
    for shift in (4, 2, 1):
        other = [pltpu.roll(v, shift, axis=0) for v in vals]
        vals = _descending([jnp.maximum(vals[k], other[PEER_TOPK - 1 - k]) for k in range(PEER_TOPK)], _BITONIC16)
    return vals


def _peer_prep_chunk(s0, s1):
    groups = PEER_NKEYS // SUBLANES
    a = _top16_sorted(s0.reshape(groups, SUBLANES, LANES))
    b = _top16_sorted(s1.reshape(groups, SUBLANES, LANES))
    sub = lax.broadcasted_iota(jnp.int32, (SUBLANES, LANES), 0)
    cand = []
    for v in range(_CAND_VREGS):
        acc = jnp.full((SUBLANES, LANES), -jnp.inf, jnp.float32)
        for k, (p, q) in enumerate(_CAND_PAIRS[v * SUBLANES:(v + 1) * SUBLANES]):
            acc = jnp.where(sub == k, a[p] + b[q], acc)
        cand.append(acc)
    best = _top_values(jnp.stack(cand), PEER_TOPK)
    z = jnp.ones((SUBLANES, LANES), jnp.float32)
    for r in range(1, PEER_TOPK):
        z = z + jnp.exp(best[r] - best[0])
    tau = best[PEER_TOPK - 1][0:1, :]
    floor = jnp.exp((tau - b[0][0:1, :]) - s0) * _TIE_SLACK
    rowgate = jnp.exp(s0 - a[0][0:1, :]) * (0.5 / z[0:1, :])
    colgate = jnp.exp(s1 - b[0][0:1, :])
    return floor, rowgate, colgate


def _row(words, k):
    return jnp.broadcast_to(words[k:k + 1, :], (PEER_NKEYS, LANES))


def _peer_gate_rows(eb, rows, floor_ref, rowg_ref, colg_ref, tile_ref, gt_ref):
    base = rows[0] - rows[0] % SUBLANES
    assert all(base <= ii < base + SUBLANES for ii in rows)
    i0 = pl.multiple_of(eb * (PEER_EB // PEER_NKEYS) + base, SUBLANES)
    for c in range(PEER_CHUNKS):
        w = [jnp.zeros((PEER_NKEYS, LANES), jnp.float32) for _ in rows]
        for h in range(PEER_HEADS):
            colgate = colg_ref[h, c]
            floors = floor_ref[h, c, pl.ds(i0, SUBLANES), :]
            rowgates = rowg_ref[h, c, pl.ds(i0, SUBLANES), :]
            for n, ii in enumerate(rows):
                w[n] = w[n] + jnp.where(colgate >= _row(floors, ii - base), colgate, 0.0) * _row(rowgates, ii - base)
        for n, ii in enumerate(rows):
            gt_ref[ii * PEER_NKEYS:(ii + 1) * PEER_NKEYS, c * LANES:(c + 1) * LANES] = (w[n] * tile_ref[ii, c]).astype(jnp.bfloat16)


def _peer_kernel(x_ref, g_ref, wqt_ref, sk_ref, u_ref, vt_ref, fg_ref, o_ref,
                 xnt_ref, qt_ref, tile_ref, floor_ref, rowg_ref, colg_ref, gt_ref, acc_ref, *, final_norm):
    eb = pl.program_id(2)
    neb = pl.num_programs(2)

    @pl.when(eb == 0)
    def _prep():
        xn = _rms(x_ref[...], g_ref[...])
        xnt_ref[...] = xn.T.astype(jnp.bfloat16)
        acc_ref[...] = jnp.zeros_like(acc_ref)

        q_rows = 2 * PEER_HEADS * PEER_HALF // PEER_Q_PIECES
        for k in range(PEER_Q_PIECES):
            rows = slice(k * q_rows, (k + 1) * q_rows)
            qt_ref[rows, :] = _dot(wqt_ref[rows, :], xnt_ref[...]).astype(jnp.bfloat16)

        for hp in range(2 * PEER_HEADS):
            s = _dot(sk_ref[hp], qt_ref[hp * PEER_HALF:(hp + 1) * PEER_HALF, :])
            for c in range(PEER_CHUNKS):
                tile_ref[hp, c] = s[:, c * LANES:(c + 1) * LANES]

        def chunk_pair(idx, carry):
            h = idx // (PEER_CHUNKS // 2)
            for c in (2 * (idx % (PEER_CHUNKS // 2)), 2 * (idx % (PEER_CHUNKS // 2)) + 1):
                floor_ref[h, c], rowg_ref[h, c], colg_ref[h, c] = _peer_prep_chunk(
                    tile_ref[2 * h, c], tile_ref[2 * h + 1, c])
            return carry

        lax.fori_loop(0, PEER_HEADS * PEER_CHUNKS // 2, chunk_pair, 0)

    for ii in range(PEER_EB // PEER_NKEYS):
        hidden = _dot(u_ref[ii * PEER_NKEYS:(ii + 1) * PEER_NKEYS, :], xnt_ref[...])
        act = hidden + hidden * lax.erf(hidden * (2.0 ** -0.5))
        for c in range(PEER_CHUNKS):
            tile_ref[ii, c] = act[:, c * LANES:(c + 1) * LANES]

    for r0 in range(0, PEER_EB // PEER_NKEYS, PEER_GATE_ROWS):
        _peer_gate_rows(eb, list(range(r0, r0 + PEER_GATE_ROWS)), floor_ref, rowg_ref, colg_ref, tile_ref, gt_ref)
    acc_ref[...] += _dot(vt_ref[...], gt_ref[...])

    @pl.when(eb == neb - 1)
    def _finish():
        y = x_ref[...] + acc_ref[...].T
        if final_norm:
            y = _rms(y, fg_ref[...])
        o_ref[...] = y


def _peer(x, g, wqt, sk, u, vt, fg, final_norm):
    b, s, _ = x.shape
    tok = pl.BlockSpec((None, PEER_TOK, D_MODEL), lambda bi, i, e: (bi, i, 0))
    full = lambda shape: pl.BlockSpec(shape, lambda bi, i, e: (0,) * len(shape), pipeline_mode=pl.Buffered(1))
    gate_scratch = pltpu.VMEM((PEER_HEADS, PEER_CHUNKS, PEER_NKEYS, LANES), jnp.float32)
    return pl.pallas_call(
        functools.partial(_peer_kernel, final_norm=final_norm),
        name="peer",
        grid=(b, s // PEER_TOK, PEER_EXPERTS // PEER_EB),
        in_specs=[
            tok, full((1, D_MODEL)), full((2 * PEER_HEADS * PEER_HALF, D_MODEL)),
            full((2 * PEER_HEADS, PEER_NKEYS, PEER_HALF)),
            pl.BlockSpec((PEER_EB, D_MODEL), lambda bi, i, e: (e, 0)),
            pl.BlockSpec((D_MODEL, PEER_EB), lambda bi, i, e: (0, e)),
            full((1, D_MODEL)),
        ],
        out_specs=tok,
        out_shape=jax.ShapeDtypeStruct((b, s, D_MODEL), jnp.float32),
        scratch_shapes=[
            pltpu.VMEM((D_MODEL, PEER_TOK), jnp.bfloat16),
            pltpu.VMEM((2 * PEER_HEADS * PEER_HALF, PEER_TOK), jnp.bfloat16),
            pltpu.VMEM((max(2 * PEER_HEADS, PEER_EB // PEER_NKEYS), PEER_CHUNKS, PEER_NKEYS, LANES), jnp.float32),
            gate_scratch, gate_scratch, gate_scratch,
            pltpu.VMEM((PEER_EB, PEER_TOK), jnp.bfloat16),
            pltpu.VMEM((D_MODEL, PEER_TOK), jnp.float32),
        ],
        compiler_params=_params(("parallel", "parallel", "arbitrary")),
    )(x, g, wqt, sk, u, vt, fg)


def _prepare_layer(l, w_in, conv_w, swa_sink, w_out, norm_mix_g, norm_xa_g, norm_mem_g, w_xq, w_xk, w_xv, w_xo,
                   norm_ffn_g, peer_wq, peer_subkeys, peer_u, peer_v):
    bf = jnp.bfloat16
    scale = HEAD_DIM ** -0.5
    o = np.cumsum([0, NA_DIM, NA_DIM, NA_DIM, SC_DIM, SC_DIM, SC_DIM, SWA_DIM, SWA_KV_DIM, SWA_KV_DIM])
    wi = w_in[l]
    w_in_l = jnp.concatenate([
        wi[:, o[0]:o[1]] * scale, wi[:, o[1]:o[6]], _swa_regroup(wi[:, o[6]:o[7]] * scale, 1), wi[:, o[7]:o[9]],
    ], axis=1).astype(bf)
    wo = w_out[l]
    row = lambda v: v.reshape(1, -1).astype(jnp.float32)
    return dict(
        g_mix=row(norm_mix_g[l]), w_in=w_in_l, conv_w=conv_w[l].astype(jnp.float32),
        sink=swa_sink[l].astype(jnp.float32),
        wna=wo[:NA_DIM].astype(bf), wsc=wo[NA_DIM:NA_DIM + SC_DIM].astype(bf),
        wsw=_swa_regroup(wo[NA_DIM + SC_DIM:], 0).astype(bf),
        g_xa=row(norm_xa_g[l]), g_mem=row(norm_mem_g[l]),
        wxq=w_xq[l].astype(bf), wxk=w_xk[l].astype(bf), wxv=w_xv[l].astype(bf), wxo=w_xo[l].astype(bf),
        g_ffn=row(norm_ffn_g[l]), wqt=peer_wq[l].T.astype(bf),
        sk=peer_subkeys[l].reshape(2 * PEER_HEADS, PEER_NKEYS, PEER_HALF).astype(bf),
        u=peer_u[l].astype(bf), vt=peer_v[l].T.astype(bf),
    )


def _trunk(x, mem, layers, na_bias, swa_bias, final_g):
    depth = len(layers)
    for l, p in enumerate(layers):
        naq, nak, nav, u, bgate, swq, swk, swv = _mix_in(x, p["g_mix"], p["w_in"])
        yna = _na_attn(naq, nak, nav, na_bias[l])
        ysw = _swa_attn(p["sink"], swq, swk, swv, swa_bias)
        km, vm = _mem_kv(mem, p["g_mem"], p["wxk"], p["wxv"])
        x = _mix_out(x, yna, ysw, u, bgate, p["conv_w"], p["wna"], p["wsc"], p["wsw"],
                     p["g_xa"], p["wxq"], km, vm, p["wxo"])
        x = _peer(x, p["g_ffn"], p["wqt"], p["sk"], p["u"], p["vt"], final_g, final_norm=(l == depth - 1))
    return x


def kernel(x_prompt, x_sample, mem_prompt, mem_sample, norm_mix_g, w_in, na_rpb, conv_w, swa_sink, t5_bias, w_out,
           norm_xa_g, norm_mem_g, w_xq, w_xk, w_xv, w_xo, norm_ffn_g, peer_wq, peer_subkeys, peer_u, peer_v, final_g):
    depth = w_in.shape[0]
    layers = [_prepare_layer(l, w_in, conv_w, swa_sink, w_out, norm_mix_g, norm_xa_g, norm_mem_g, w_xq, w_xk, w_xv,
                             w_xo, norm_ffn_g, peer_wq, peer_subkeys, peer_u, peer_v) for l in range(depth)]
    fg = final_g.reshape(1, -1).astype(jnp.float32)
    tables = {}
    outs = []
    for x, mem in ((x_prompt, mem_prompt), (x_sample, mem_sample)):
        s = x.shape[1]
        if s not in tables:
            tables[s] = ([_na_bias_tables(na_rpb[l], s) for l in range(depth)], _swa_bias_tables(t5_bias, s))
        outs.append(_trunk(x, mem, layers, *tables[s], fg))
    return tuple(outs)
```

```python
import functools

import numpy as np
import jax
import jax.numpy as jnp
from jax import lax
from jax.experimental import pallas as pl
from jax.experimental.pallas import tpu as pltpu

D_MODEL = 1024
GRID_W = 64
HEAD_DIM = 64
NA_HEADS = 6
NA_WIN_H = 8
NA_WIN_W = 16
NA_DIM = NA_HEADS * HEAD_DIM
SC_DIM = 256
SWA_HEADS = 6
SWA_KV_HEADS = 2
SWA_GROUP = SWA_HEADS // SWA_KV_HEADS
SWA_WINDOW = 128
SWA_DIM = SWA_HEADS * HEAD_DIM
SWA_KV_DIM = SWA_KV_HEADS * HEAD_DIM
T5_BUCKETS = 32
T5_MAX_DIST = 128
IN_DIM = 3 * NA_DIM + 3 * SC_DIM + SWA_DIM + 2 * SWA_KV_DIM
XA_HEADS = 4
XA_HEAD_DIM = 128
XA_DIM = XA_HEADS * XA_HEAD_DIM
PEER_HEADS = 8
PEER_NKEYS = 128
PEER_EXPERTS = PEER_NKEYS * PEER_NKEYS
PEER_TOPK = 16
PEER_HALF = 128
RMS_EPS = 1e-6
NEG_INF = -1e30

LANES = 128
SUBLANES = 8
VMEM_LIMIT = 56 * 1024 * 1024

TOK_TILE = 512
NA_Q_ROWS = 4
NA_K_ROWS = NA_Q_ROWS + NA_WIN_H
NA_Q = NA_Q_ROWS * GRID_W
NA_K = NA_K_ROWS * GRID_W
SWA_Q = 256
SWA_K = SWA_Q + 2 * SWA_WINDOW
PEER_TOK = 512
PEER_EB = 2048
PEER_CHUNKS = PEER_TOK // LANES
PEER_H_ROWS = 4
PEER_GATE_ROWS = 2
PEER_Q_PIECES = 4

_NT = (((1,), (1,)), ((), ()))


def _rms(xf, g):
    return xf * lax.rsqrt(jnp.mean(xf * xf, axis=-1, keepdims=True) + RMS_EPS) * g


def _dot(a, b):
    return jnp.dot(a, b, preferred_element_type=jnp.float32)


def _dot_nt(a, b):
    return lax.dot_general(a, b, _NT, preferred_element_type=jnp.float32)


def _params(sem):
    return pltpu.CompilerParams(dimension_semantics=sem, vmem_limit_bytes=VMEM_LIMIT)


def _mix_in_kernel(x_ref, g_ref, w_ref, naq_ref, nak_ref, nav_ref, u_ref, b_ref, swq_ref, swk_ref, swv_ref):
    xn = _rms(x_ref[...], g_ref[...]).astype(jnp.bfloat16)
    z = _dot(xn, w_ref[...])
    o = 0
    naq_ref[...] = z[:, o:o + NA_DIM].astype(jnp.bfloat16); o += NA_DIM
    nak_ref[...] = z[:, o:o + NA_DIM].astype(jnp.bfloat16); o += NA_DIM
    nav_ref[...] = z[:, o:o + NA_DIM].astype(jnp.bfloat16); o += NA_DIM
    b_ref[...] = z[:, o:o + SC_DIM]; o += SC_DIM
    u_ref[...] = z[:, o:o + SC_DIM] * z[:, o + SC_DIM:o + 2 * SC_DIM]; o += 2 * SC_DIM
    swq_ref[...] = z[:, o:o + SWA_DIM].astype(jnp.bfloat16); o += SWA_DIM
    swk_ref[...] = z[:, o:o + SWA_KV_DIM].astype(jnp.bfloat16); o += SWA_KV_DIM
    swv_ref[...] = z[:, o:o + SWA_KV_DIM].astype(jnp.bfloat16)


def _mix_in(x, g, w):
    b, s, _ = x.shape
    tok = lambda d: pl.BlockSpec((None, TOK_TILE, d), lambda bi, i: (bi, i, 0))
    full = lambda shape: pl.BlockSpec(shape, lambda bi, i: (0,) * len(shape))
    widths = [(NA_DIM, jnp.bfloat16)] * 3 + [(SC_DIM, jnp.float32)] * 2 + \
             [(SWA_DIM, jnp.bfloat16), (SWA_KV_DIM, jnp.bfloat16), (SWA_KV_DIM, jnp.bfloat16)]
    return pl.pallas_call(
        _mix_in_kernel,
        name="mix_in",
        grid=(b, s // TOK_TILE),
        in_specs=[tok(D_MODEL), full((1, D_MODEL)), full((D_MODEL, IN_DIM))],
        out_specs=[tok(d) for d, _ in widths],
        out_shape=[jax.ShapeDtypeStruct((b, s, d), t) for d, t in widths],
        compiler_params=_params(("parallel", "parallel")),
    )(x, g, w)


def _pair_heads(q_pair, k_pair, v_pair, bias_lo, bias_hi, extra_lo=None, extra_hi=None):
    low = lax.broadcasted_iota(jnp.int32, (1, LANES), 1) < HEAD_DIM
    outs = []
    for keep, bias, extra in ((low, bias_lo, extra_lo), (~low, bias_hi, extra_hi)):
        qm = jnp.where(keep, q_pair, jnp.zeros_like(q_pair))
        logits = _dot_nt(qm, k_pair) + bias
        mx = jnp.max(logits, axis=-1, keepdims=True)
        if extra is not None:
            mx = jnp.maximum(mx, extra)
        p = jnp.exp(logits - mx)
        den = jnp.sum(p, axis=-1, keepdims=True)
        if extra is not None:
            den = den + jnp.exp(extra - mx)
        outs.append(_dot(p.astype(jnp.bfloat16), v_pair) / den)
    return jnp.where(low, outs[0], outs[1])


def _na_kernel(q_ref, k_ref, v_ref, bias_ref, o_ref):
    m = pl.program_id(1)
    rows = k_ref.shape[0] // GRID_W
    start_row = jnp.clip(m * NA_Q_ROWS - NA_WIN_H // 2, 0, rows - NA_K_ROWS)
    start = pl.multiple_of(start_row * GRID_W, GRID_W * NA_Q_ROWS)
    kwin = k_ref[pl.ds(start, NA_K), :]
    vwin = v_ref[pl.ds(start, NA_K), :]
    q = q_ref[...]
    for g in range(NA_HEADS // 2):
        sl = slice(g * LANES, (g + 1) * LANES)
        out = _pair_heads(q[:, sl], kwin[:, sl], vwin[:, sl], bias_ref[2 * g], bias_ref[2 * g + 1])
        o_ref[:, sl] = out.astype(o_ref.dtype)


def _na_case(m, nsteps):
    return jnp.where(m == 0, 0, jnp.where(m == nsteps - 1, 2, 1))


def _na_attn(q, k, v, bias):
    b, s, _ = q.shape
    nsteps = s // NA_Q
    return pl.pallas_call(
        _na_kernel,
        name="na_attn",
        grid=(b, nsteps),
        in_specs=[
            pl.BlockSpec((None, NA_Q, NA_DIM), lambda bi, m: (bi, m, 0)),
            pl.BlockSpec((None, s, NA_DIM), lambda bi, m: (bi, 0, 0)),
            pl.BlockSpec((None, s, NA_DIM), lambda bi, m: (bi, 0, 0)),
            pl.BlockSpec((None, NA_HEADS, NA_Q, NA_K), lambda bi, m: (_na_case(m, nsteps), 0, 0, 0)),
        ],
        out_specs=pl.BlockSpec((None, NA_Q, NA_DIM), lambda bi, m: (bi, m, 0)),
        out_shape=jax.ShapeDtypeStruct((b, s, NA_DIM), jnp.bfloat16),
        compiler_params=_params(("parallel", "arbitrary")),
    )(q, k, v, bias)


def _na_bias_tables(rpb, s):
    rows = s // GRID_W
    nsteps = rows // NA_Q_ROWS

    c = np.arange(GRID_W)[:, None]
    kc = np.arange(GRID_W)[None, :]
    c0 = np.clip(c - NA_WIN_W // 2, 0, GRID_W - NA_WIN_W)
    valid_c = (kc >= c0) & (kc < c0 + NA_WIN_W)
    sel_c = (kc - c + NA_WIN_W - 1)[..., None] == np.arange(2 * NA_WIN_W - 1)

    def row_geometry(m):
        start_row = int(np.clip(m * NA_Q_ROWS - NA_WIN_H // 2, 0, rows - NA_K_ROWS))
        r = (m * NA_Q_ROWS + np.arange(NA_Q_ROWS))[:, None]
        kr = (start_row + np.arange(NA_K_ROWS))[None, :]
        r0 = np.clip(r - NA_WIN_H // 2, 0, rows - NA_WIN_H)
        valid_r = (kr >= r0) & (kr < r0 + NA_WIN_H)
        sel_r = (kr - r + NA_WIN_H - 1)[..., None] == np.arange(2 * NA_WIN_H - 1)
        return valid_r, sel_r

    interior = row_geometry(1)
    for m in range(1, nsteps - 1):
        assert all(np.array_equal(a, b_) for a, b_ in zip(row_geometry(m), interior))
    tabs = []
    for valid_r, sel_r in (row_geometry(0), interior, row_geometry(nsteps - 1)):
        t = jnp.einsum("rka,abh,cjb->hrckj", sel_r.astype(np.float32), rpb.astype(jnp.float32),
                       sel_c.astype(np.float32), precision=lax.Precision.HIGHEST)
        valid = valid_r[:, None, :, None] & valid_c[None, :, None, :]
        tabs.append(jnp.where(valid[None], t, NEG_INF).reshape(NA_HEADS, NA_Q, NA_K))
    return jnp.stack(tabs)


def _swa_kernel(sink_ref, q_ref, k_ref, v_ref, bias_ref, o_ref):
    n = pl.program_id(1)
    s = k_ref.shape[0]
    start = pl.multiple_of(jnp.clip(n * SWA_Q - SWA_WINDOW, 0, s - SWA_K), SWA_WINDOW)
    kwin = k_ref[pl.ds(start, SWA_K), :]
    vwin = v_ref[pl.ds(start, SWA_K), :]
    q = q_ref[...]
    for g in range(SWA_GROUP):
        sl = slice(g * LANES, (g + 1) * LANES)
        out = _pair_heads(q[:, sl], kwin, vwin, bias_ref[2 * g], bias_ref[2 * g + 1],
                          sink_ref[g], sink_ref[g + SWA_GROUP])
        o_ref[:, sl] = out.astype(o_ref.dtype)


def _swa_attn(sink, q, k, v, bias):
    b, s, _ = q.shape
    nsteps = s // SWA_Q
    return pl.pallas_call(
        _swa_kernel,
        name="swa_attn",
        grid=(b, nsteps),
        in_specs=[
            pl.BlockSpec(memory_space=pltpu.SMEM),
            pl.BlockSpec((None, SWA_Q, SWA_DIM), lambda bi, n: (bi, n, 0)),
            pl.BlockSpec((None, s, SWA_KV_DIM), lambda bi, n: (bi, 0, 0)),
            pl.BlockSpec((None, s, SWA_KV_DIM), lambda bi, n: (bi, 0, 0)),
            pl.BlockSpec((None, SWA_HEADS, SWA_Q, SWA_K), lambda bi, n: (_na_case(n, nsteps), 0, 0, 0)),
        ],
        out_specs=pl.BlockSpec((None, SWA_Q, SWA_DIM), lambda bi, n: (bi, n, 0)),
        out_shape=jax.ShapeDtypeStruct((b, s, SWA_DIM), jnp.bfloat16),
        compiler_params=_params(("parallel", "arbitrary")),
    )(sink, q, k, v, bias)


def _t5_bucket(rel):
    nb = T5_BUCKETS // 2
    max_exact = nb // 2
    ret = (rel > 0).astype(np.int32) * nb
    n = np.abs(rel)
    large = max_exact + (np.log(np.maximum(n, 1) / max_exact) / np.log(T5_MAX_DIST / max_exact)
                         * (nb - max_exact)).astype(np.int32)
    large = np.minimum(large, nb - 1)
    return (ret + np.where(n < max_exact, n, large)).astype(np.int32)


_SWA_SLOT_HEADS = [h for g in range(SWA_GROUP) for h in (g, g + SWA_GROUP)]


def _swa_regroup(w, axis):
    blocks = [lax.slice_in_dim(w, h * HEAD_DIM, (h + 1) * HEAD_DIM, axis=axis) for h in _SWA_SLOT_HEADS]
    return jnp.concatenate(blocks, axis=axis)


def _swa_bias_tables(t5_bias, s):
    offsets = np.arange(-SWA_WINDOW, SWA_WINDOW + 1)
    sel = _t5_bucket(offsets)[:, None] == np.arange(T5_BUCKETS)
    bias_off = jnp.dot(sel.astype(np.float32), t5_bias.astype(jnp.float32), precision=lax.Precision.HIGHEST)
    bias_off = jnp.stack([bias_off[:, h] for h in _SWA_SLOT_HEADS])
    nsteps = s // SWA_Q
    period = SWA_Q + SWA_K - 1
    tabs = []
    for n in (0, 1, nsteps - 1):
        start = int(np.clip(n * SWA_Q - SWA_WINDOW, 0, s - SWA_K))
        delta = n * SWA_Q - start
        lead = SWA_Q - 1 + delta - SWA_WINDOW
        g = jnp.concatenate([jnp.full((SWA_HEADS, lead), NEG_INF, jnp.float32), bias_off,
                             jnp.full((SWA_HEADS, period - lead - 2 * SWA_WINDOW - 1), NEG_INF, jnp.float32)], axis=1)
        hankel = jnp.tile(g, (1, SWA_Q + 1))[:, :SWA_Q * (period + 1)].reshape(SWA_HEADS, SWA_Q, period + 1)
        tabs.append(jnp.flip(hankel[:, :, :SWA_K], axis=1))
    return jnp.stack(tabs)


def _mem_kv_kernel(mem_ref, g_ref, wk_ref, wv_ref, k_ref, v_ref):
    mn = _rms(mem_ref[...], g_ref[...]).astype(jnp.bfloat16)
    k_ref[...] = _dot(mn, wk_ref[...]).astype(jnp.bfloat16)
    v_ref[...] = _dot(mn, wv_ref[...]).astype(jnp.bfloat16)


def _mem_kv(mem, g, wk, wv):
    b, m, _ = mem.shape
    full = lambda shape: pl.BlockSpec(shape, lambda bi: (0,) * len(shape))
    return pl.pallas_call(
        _mem_kv_kernel,
        name="mem_kv",
        grid=(b,),
        in_specs=[pl.BlockSpec((None, m, D_MODEL), lambda bi: (bi, 0, 0)), full((1, D_MODEL)),
                  full((D_MODEL, XA_DIM)), full((D_MODEL, XA_DIM))],
        out_specs=[pl.BlockSpec((None, m, XA_DIM), lambda bi: (bi, 0, 0))] * 2,
        out_shape=[jax.ShapeDtypeStruct((b, m, XA_DIM), jnp.bfloat16)] * 2,
        compiler_params=_params(("parallel",)),
    )(mem, g, wk, wv)


def _mix_out_kernel(x_ref, yna_ref, ysw_ref, u_ref, uprev_ref, unext_ref, b_ref, cw_ref,
                    wna_ref, wsc_ref, wsw_ref, g_ref, wq_ref, km_ref, vm_ref, wo_ref, o_ref):
    i = pl.program_id(1)
    nt = pl.num_programs(1)
    u = u_ref[...]
    t = u.shape[0]
    row = lax.broadcasted_iota(jnp.int32, (t, 1), 0)
    prev_row = jnp.where(i > 0, uprev_ref[SUBLANES - 1:SUBLANES, :], 0.0)
    next_row = jnp.where(i < nt - 1, unext_ref[0:1, :], 0.0)
    u_m1 = jnp.where(row == 0, prev_row, pltpu.roll(u, 1, axis=0))
    u_p1 = jnp.where(row == t - 1, next_row, pltpu.roll(u, t - 1, axis=0))
    ysc = b_ref[...] * (u_m1 * cw_ref[0:1, :] + u * cw_ref[1:2, :] + u_p1 * cw_ref[2:3, :])
    y = _dot(yna_ref[...], wna_ref[...]) + _dot(ysc.astype(jnp.bfloat16), wsc_ref[...]) \
        + _dot(ysw_ref[...], wsw_ref[...])
    x1 = x_ref[...] + y

    xn = _rms(x1, g_ref[...]).astype(jnp.bfloat16)
    q = _dot(xn, wq_ref[...])
    outs = []
    for h in range(XA_HEADS):
        sl = slice(h * XA_HEAD_DIM, (h + 1) * XA_HEAD_DIM)
        logits = _dot_nt(q[:, sl].astype(jnp.bfloat16), km_ref[:, sl]) * (XA_HEAD_DIM ** -0.5)
        mx = jnp.max(logits, axis=-1, keepdims=True)
        p = jnp.exp(logits - mx)
        den = jnp.sum(p, axis=-1, keepdims=True)
        outs.append((_dot(p.astype(jnp.bfloat16), vm_ref[:, sl]) / den).astype(jnp.bfloat16))
    o = jnp.concatenate(outs, axis=-1)
    o_ref[...] = x1 + _dot(o, wo_ref[...])


def _mix_out(x, yna, ysw, u, bgate, cw, wna, wsc, wsw, g, wq, km, vm, wo):
    b, s, _ = x.shape
    nt = s // TOK_TILE
    halo_blocks = TOK_TILE // SUBLANES
    last_halo = s // SUBLANES - 1
    tok = lambda d: pl.BlockSpec((None, TOK_TILE, d), lambda bi, i: (bi, i, 0))
    full = lambda shape: pl.BlockSpec(shape, lambda bi, i: (0,) * len(shape))
    mem = pl.BlockSpec((None, km.shape[1], XA_DIM), lambda bi, i: (bi, 0, 0))
    return pl.pallas_call(
        _mix_out_kernel,
        name="mix_out",
        grid=(b, nt),
        in_specs=[
            tok(D_MODEL), tok(NA_DIM), tok(SWA_DIM), tok(SC_DIM),
            pl.BlockSpec((None, SUBLANES, SC_DIM), lambda bi, i: (bi, jnp.maximum(i * halo_blocks - 1, 0), 0)),
            pl.BlockSpec((None, SUBLANES, SC_DIM), lambda bi, i: (bi, jnp.minimum((i + 1) * halo_blocks, last_halo), 0)),
            tok(SC_DIM), full((3, SC_DIM)),
            full((NA_DIM, D_MODEL)), full((SC_DIM, D_MODEL)), full((SWA_DIM, D_MODEL)),
            full((1, D_MODEL)), full((D_MODEL, XA_DIM)), mem, mem, full((XA_DIM, D_MODEL)),
        ],
        out_specs=tok(D_MODEL),
        out_shape=jax.ShapeDtypeStruct((b, s, D_MODEL), jnp.float32),
        compiler_params=_params(("parallel", "parallel")),
    )(x, yna, ysw, u, u, u, bgate, cw, wna, wsc, wsw, g, wq, km, vm, wo)


_CAND_PAIRS = [(p, q) for p in range(PEER_TOPK) for q in range(PEER_TOPK) if (p + 1) * (q + 1) <= PEER_TOPK]
_CAND_VREGS = -(-len(_CAND_PAIRS) // SUBLANES)


def _allmax_sublanes(v):
    for shift in (4, 2, 1):
        v = jnp.maximum(v, pltpu.roll(v, shift, axis=0))
    return v


def _top_values(s3, count):
    vals = []
    for r in range(count):
        m = _allmax_sublanes(jnp.max(s3, axis=0))
        vals.append(m)
        if r + 1 < count:
            s3 = jnp.where(s3 == m[None], -jnp.inf, s3)
    return vals


def _batcher_pairs(n):
    pairs = []
    p = 1
    while p < n:
        k = p
        while k >= 1:
            for j in range(k % p, n - k, 2 * k):
                for i in range(min(k, n - j - k)):
                    if (i + j) // (2 * p) == (i + j + k) // (2 * p):
                        pairs.append((i + j, i + j + k))
            k //= 2
        p *= 2
    return pairs


_TIE_SLACK = 1.0 - 2.0 ** -18
_SORT16 = _batcher_pairs(PEER_TOPK)
_BITONIC16 = [(i, i + k) for k in (8, 4, 2, 1) for i in range(PEER_TOPK) if not i & k]


def _descending(vals, pairs):
    vals = list(vals)
    for i, j in pairs:
        vals[i], vals[j] = jnp.maximum(vals[i], vals[j]), jnp.minimum(vals[i], vals[j])
    return vals


def _top16_sorted(s3):
    vals = _descending([s3[k] for k in range(PEER_TOPK)], _SORT16)
    for shift in (4, 2, 1):
        other = [pltpu.roll(v, shift, axis=0) for v in vals]
        vals = _descending([jnp.maximum(vals[k], other[PEER_TOPK - 1 - k]) for k in range(PEER_TOPK)], _BITONIC16)
    return vals


def _peer_prep_chunk(s0, s1):
    groups = PEER_NKEYS // SUBLANES
    a = _top16_sorted(s0.reshape(groups, SUBLANES, LANES))
    b = _top16_sorted(s1.reshape(groups, SUBLANES, LANES))
    sub = lax.broadcasted_iota(jnp.int32, (SUBLANES, LANES), 0)
    cand = []
    for v in range(_CAND_VREGS):
        acc = jnp.full((SUBLANES, LANES), -jnp.inf, jnp.float32)
        for k, (p, q) in enumerate(_CAND_PAIRS[v * SUBLANES:(v + 1) * SUBLANES]):
            acc = jnp.where(sub == k, a[p] + b[q], acc)
        cand.append(acc)
    best = _top_values(jnp.stack(cand), PEER_TOPK)
    z = jnp.ones((SUBLANES, LANES), jnp.float32)
    for r in range(1, PEER_TOPK):
        z = z + jnp.exp(best[r] - best[0])
    tau = best[PEER_TOPK - 1][0:1, :]
    floor = jnp.exp((tau - b[0][0:1, :]) - s0) * _TIE_SLACK
    rowgate = jnp.exp(s0 - a[0][0:1, :]) * (0.5 / z[0:1, :])
    colgate = jnp.exp(s1 - b[0][0:1, :])
    return floor, rowgate, colgate


def _row(words, k):
    return jnp.broadcast_to(words[k:k + 1, :], (PEER_NKEYS, LANES))


def _peer_gate_rows(eb, rows, floor_ref, rowg_ref, colg_ref, tile_ref, gt_ref):
    base = rows[0] - rows[0] % SUBLANES
    assert all(base <= ii < base + SUBLANES for ii in rows)
    i0 = pl.multiple_of(eb * (PEER_EB // PEER_NKEYS) + base, SUBLANES)
    for c in range(PEER_CHUNKS):
        w = [jnp.zeros((PEER_NKEYS, LANES), jnp.float32) for _ in rows]
        for h in range(PEER_HEADS):
            colgate = colg_ref[h, c]
            floors = floor_ref[h, c, pl.ds(i0, SUBLANES), :]
            rowgates = rowg_ref[h, c, pl.ds(i0, SUBLANES), :]
            for n, ii in enumerate(rows):
                w[n] = w[n] + jnp.where(colgate >= _row(floors, ii - base), colgate, 0.0) * _row(rowgates, ii - base)
        for n, ii in enumerate(rows):
            gt_ref[ii * PEER_NKEYS:(ii + 1) * PEER_NKEYS, c * LANES:(c + 1) * LANES] = (w[n] * tile_ref[ii, c]).astype(jnp.bfloat16)


def _peer_kernel(x_ref, g_ref, wqt_ref, sk_ref, u_ref, vt_ref, fg_ref, o_ref,
                 xnt_ref, qt_ref, tile_ref, floor_ref, rowg_ref, colg_ref, gt_ref, acc_ref, *, final_norm):
    eb = pl.program_id(2)
    neb = pl.num_programs(2)

    @pl.when(eb == 0)
    def _prep():
        xn = _rms(x_ref[...], g_ref[...])
        xnt_ref[...] = xn.T.astype(jnp.bfloat16)
        acc_ref[...] = jnp.zeros_like(acc_ref)

        q_rows = 2 * PEER_HEADS * PEER_HALF // PEER_Q_PIECES
        for k in range(PEER_Q_PIECES):
            rows = slice(k * q_rows, (k + 1) * q_rows)
            qt_ref[rows, :] = _dot(wqt_ref[rows, :], xnt_ref[...]).astype(jnp.bfloat16)

        for hp in range(2 * PEER_HEADS):
            s = _dot(sk_ref[hp], qt_ref[hp * PEER_HALF:(hp + 1) * PEER_HALF, :])
            for c in range(PEER_CHUNKS):
                tile_ref[hp, c] = s[:, c * LANES:(c + 1) * LANES]

        def chunk_pair(idx, carry):
            h = idx // (PEER_CHUNKS // 2)
            for c in (2 * (idx % (PEER_CHUNKS // 2)), 2 * (idx % (PEER_CHUNKS // 2)) + 1):
                floor_ref[h, c], rowg_ref[h, c], colg_ref[h, c] = _peer_prep_chunk(
                    tile_ref[2 * h, c], tile_ref[2 * h + 1, c])
            return carry

        lax.fori_loop(0, PEER_HEADS * PEER_CHUNKS // 2, chunk_pair, 0)

    for i0 in range(0, PEER_EB // PEER_NKEYS, PEER_H_ROWS):
        hidden = _dot(u_ref[i0 * PEER_NKEYS:(i0 + PEER_H_ROWS) * PEER_NKEYS, :], xnt_ref[...])
        act = hidden + hidden * lax.erf(hidden * (2.0 ** -0.5))
        for k in range(PEER_H_ROWS):
            for c in range(PEER_CHUNKS):
                tile_ref[i0 + k, c] = act[k * PEER_NKEYS:(k + 1) * PEER_NKEYS, c * LANES:(c + 1) * LANES]

    for r0 in range(0, PEER_EB // PEER_NKEYS, PEER_GATE_ROWS):
        _peer_gate_rows(eb, list(range(r0, r0 + PEER_GATE_ROWS)), floor_ref, rowg_ref, colg_ref, tile_ref, gt_ref)
    acc_ref[...] += _dot(vt_ref[...], gt_ref[...])

    @pl.when(eb == neb - 1)
    def _finish():
        y = x_ref[...] + acc_ref[...].T
        if final_norm:
            y = _rms(y, fg_ref[...])
        o_ref[...] = y


def _peer(x, g, wqt, sk, u, vt, fg, final_norm):
    b, s, _ = x.shape
    tok = pl.BlockSpec((None, PEER_TOK, D_MODEL), lambda bi, i, e: (bi, i, 0))
    full = lambda shape: pl.BlockSpec(shape, lambda bi, i, e: (0,) * len(shape), pipeline_mode=pl.Buffered(1))
    gate_scratch = pltpu.VMEM((PEER_HEADS, PEER_CHUNKS, PEER_NKEYS, LANES), jnp.float32)
    return pl.pallas_call(
        functools.partial(_peer_kernel, final_norm=final_norm),
        name="peer",
        grid=(b, s // PEER_TOK, PEER_EXPERTS // PEER_EB),
        in_specs=[
            tok, full((1, D_MODEL)), full((2 * PEER_HEADS * PEER_HALF, D_MODEL)),
            full((2 * PEER_HEADS, PEER_NKEYS, PEER_HALF)),
            pl.BlockSpec((PEER_EB, D_MODEL), lambda bi, i, e: (e, 0)),
            pl.BlockSpec((D_MODEL, PEER_EB), lambda bi, i, e: (0, e)),
            full((1, D_MODEL)),
        ],
        out_specs=tok,
        out_shape=jax.ShapeDtypeStruct((b, s, D_MODEL), jnp.float32),
        scratch_shapes=[
            pltpu.VMEM((D_MODEL, PEER_TOK), jnp.bfloat16),
            pltpu.VMEM((2 * PEER_HEADS * PEER_HALF, PEER_TOK), jnp.bfloat16),
            pltpu.VMEM((max(2 * PEER_HEADS, PEER_EB // PEER_NKEYS), PEER_CHUNKS, PEER_NKEYS, LANES), jnp.float32),
            gate_scratch, gate_scratch, gate_scratch,
            pltpu.VMEM((PEER_EB, PEER_TOK), jnp.bfloat16),
            pltpu.VMEM((D_MODEL, PEER_TOK), jnp.float32),
        ],
        compiler_params=_params(("parallel", "parallel", "arbitrary")),
    )(x, g, wqt, sk, u, vt, fg)


def _prepare_layer(l, w_in, conv_w, swa_sink, w_out, norm_mix_g, norm_xa_g, norm_mem_g, w_xq, w_xk, w_xv, w_xo,
                   norm_ffn_g, peer_wq, peer_subkeys, peer_u, peer_v):
    bf = jnp.bfloat16
    scale = HEAD_DIM ** -0.5
    o = np.cumsum([0, NA_DIM, NA_DIM, NA_DIM, SC_DIM, SC_DIM, SC_DIM, SWA_DIM, SWA_KV_DIM, SWA_KV_DIM])
    wi = w_in[l]
    w_in_l = jnp.concatenate([
        wi[:, o[0]:o[1]] * scale, wi[:, o[1]:o[6]], _swa_regroup(wi[:, o[6]:o[7]] * scale, 1), wi[:, o[7]:o[9]],
    ], axis=1).astype(bf)
    wo = w_out[l]
    row = lambda v: v.reshape(1, -1).astype(jnp.float32)
    return dict(
        g_mix=row(norm_mix_g[l]), w_in=w_in_l, conv_w=conv_w[l].astype(jnp.float32),
        sink=swa_sink[l].astype(jnp.float32),
        wna=wo[:NA_DIM].astype(bf), wsc=wo[NA_DIM:NA_DIM + SC_DIM].astype(bf),
        wsw=_swa_regroup(wo[NA_DIM + SC_DIM:], 0).astype(bf),
        g_xa=row(norm_xa_g[l]), g_mem=row(norm_mem_g[l]),
        wxq=w_xq[l].astype(bf), wxk=w_xk[l].astype(bf), wxv=w_xv[l].astype(bf), wxo=w_xo[l].astype(bf),
        g_ffn=row(norm_ffn_g[l]), wqt=peer_wq[l].T.astype(bf),
        sk=peer_subkeys[l].reshape(2 * PEER_HEADS, PEER_NKEYS, PEER_HALF).astype(bf),
        u=peer_u[l].astype(bf), vt=peer_v[l].T.astype(bf),
    )


def _trunk(x, mem, layers, na_bias, swa_bias, final_g):
    depth = len(layers)
    for l, p in enumerate(layers):
        naq, nak, nav, u, bgate, swq, swk, swv = _mix_in(x, p["g_mix"], p["w_in"])
        yna = _na_attn(naq, nak, nav, na_bias[l])
        ysw = _swa_attn(p["sink"], swq, swk, swv, swa_bias)
        km, vm = _mem_kv(mem, p["g_mem"], p["wxk"], p["wxv"])
        x = _mix_out(x, yna, ysw, u, bgate, p["conv_w"], p["wna"], p["wsc"], p["wsw"],
                     p["g_xa"], p["wxq"], km, vm, p["wxo"])
        x = _peer(x, p["g_ffn"], p["wqt"], p["sk"], p["u"], p["vt"], final_g, final_norm=(l == depth - 1))
    return x


def kernel(x_prompt, x_sample, mem_prompt, mem_sample, norm_mix_g, w_in, na_rpb, conv_w, swa_sink, t5_bias, w_out,
           norm_xa_g, norm_mem_g, w_xq, w_xk, w_xv, w_xo, norm_ffn_g, peer_wq, peer_subkeys, peer_u, peer_v, final_g):
    depth = w_in.shape[0]
    layers = [_prepare_layer(l, w_in, conv_w, swa_sink, w_out, norm_mix_g, norm_xa_g, norm_mem_g, w_xq, w_xk, w_xv,
                             w_xo, norm_ffn_g, peer_wq, peer_subkeys, peer_u, peer_v) for l in range(depth)]
    fg = final_g.reshape(1, -1).astype(jnp.float32)
    tables = {}
    outs = []
    for x, mem in ((x_prompt, mem_prompt), (x_sample, mem_sample)):
        s = x.shape[1]
        if s not in tables:
            tables[s] = ([_na_bias_tables(na_rpb[l], s) for l in range(depth)], _swa_bias_tables(t5_bias, s))
        outs.append(_trunk(x, mem, layers, *tables[s], fg))
    return tuple(outs)
```

```python
import functools

import numpy as np
import jax
import jax.numpy as jnp
from jax import lax
from jax.experimental import pallas as pl
from jax.experimental.pallas import tpu as pltpu

D_MODEL = 1024
GRID_W = 64
HEAD_DIM = 64
NA_HEADS = 6
NA_WIN_H = 8
NA_WIN_W = 16
NA_DIM = NA_HEADS * HEAD_DIM
SC_DIM = 256
SWA_HEADS = 6
SWA_KV_HEADS = 2
SWA_GROUP = SWA_HEADS // SWA_KV_HEADS
SWA_WINDOW = 128
SWA_DIM = SWA_HEADS * HEAD_DIM
SWA_KV_DIM = SWA_KV_HEADS * HEAD_DIM
T5_BUCKETS = 32
T5_MAX_DIST = 128
IN_DIM = 3 * NA_DIM + 3 * SC_DIM + SWA_DIM + 2 * SWA_KV_DIM
XA_HEADS = 4
XA_HEAD_DIM = 128
XA_DIM = XA_HEADS * XA_HEAD_DIM
PEER_HEADS = 8
PEER_NKEYS = 128
PEER_EXPERTS = PEER_NKEYS * PEER_NKEYS
PEER_TOPK = 16
PEER_HALF = 128
RMS_EPS = 1e-6
NEG_INF = -1e30

LANES = 128
SUBLANES = 8
VMEM_LIMIT = 56 * 1024 * 1024

TOK_TILE = 512
NA_Q_ROWS = 4
NA_K_ROWS = NA_Q_ROWS + NA_WIN_H
NA_Q = NA_Q_ROWS * GRID_W
NA_K = NA_K_ROWS * GRID_W
SWA_Q = 256
SWA_K = SWA_Q + 2 * SWA_WINDOW
PEER_TOK = 512
PEER_EB = 2048
PEER_CHUNKS = PEER_TOK // LANES
PEER_H_ROWS = 4
PEER_GATE_ROWS = 2
PEER_Q_PIECES = 4

_NT = (((1,), (1,)), ((), ()))


def _rms(xf, g):
    return xf * lax.rsqrt(jnp.mean(xf * xf, axis=-1, keepdims=True) + RMS_EPS) * g


def _dot(a, b):
    return jnp.dot(a, b, preferred_element_type=jnp.float32)


def _dot_nt(a, b):
    return lax.dot_general(a, b, _NT, preferred_element_type=jnp.float32)


def _params(sem):
    return pltpu.CompilerParams(dimension_semantics=sem, vmem_limit_bytes=VMEM_LIMIT)


def _mix_in_kernel(x_ref, g_ref, w_ref, naq_ref, nak_ref, nav_ref, u_ref, b_ref, swq_ref, swk_ref, swv_ref):
    xn = _rms(x_ref[...], g_ref[...]).astype(jnp.bfloat16)
    z = _dot(xn, w_ref[...])
    o = 0
    naq_ref[...] = z[:, o:o + NA_DIM].astype(jnp.bfloat16); o += NA_DIM
    nak_ref[...] = z[:, o:o + NA_DIM].astype(jnp.bfloat16); o += NA_DIM
    nav_ref[...] = z[:, o:o + NA_DIM].astype(jnp.bfloat16); o += NA_DIM
    b_ref[...] = z[:, o:o + SC_DIM]; o += SC_DIM
    u_ref[...] = z[:, o:o + SC_DIM] * z[:, o + SC_DIM:o + 2 * SC_DIM]; o += 2 * SC_DIM
    swq_ref[...] = z[:, o:o + SWA_DIM].astype(jnp.bfloat16); o += SWA_DIM
    swk_ref[...] = z[:, o:o + SWA_KV_DIM].astype(jnp.bfloat16); o += SWA_KV_DIM
    swv_ref[...] = z[:, o:o + SWA_KV_DIM].astype(jnp.bfloat16)


def _mix_in(x, g, w):
    b, s, _ = x.shape
    tok = lambda d: pl.BlockSpec((None, TOK_TILE, d), lambda bi, i: (bi, i, 0))
    full = lambda shape: pl.BlockSpec(shape, lambda bi, i: (0,) * len(shape))
    widths = [(NA_DIM, jnp.bfloat16)] * 3 + [(SC_DIM, jnp.float32)] * 2 + \
             [(SWA_DIM, jnp.bfloat16), (SWA_KV_DIM, jnp.bfloat16), (SWA_KV_DIM, jnp.bfloat16)]
    return pl.pallas_call(
        _mix_in_kernel,
        name="mix_in",
        grid=(b, s // TOK_TILE),
        in_specs=[tok(D_MODEL), full((1, D_MODEL)), full((D_MODEL, IN_DIM))],
        out_specs=[tok(d) for d, _ in widths],
        out_shape=[jax.ShapeDtypeStruct((b, s, d), t) for d, t in widths],
        compiler_params=_params(("parallel", "parallel")),
    )(x, g, w)


def _pair_heads(q_pair, k_pair, v_pair, bias_lo, bias_hi, extra_lo=None, extra_hi=None):
    low = lax.broadcasted_iota(jnp.int32, (1, LANES), 1) < HEAD_DIM
    outs = []
    for keep, bias, extra in ((low, bias_lo, extra_lo), (~low, bias_hi, extra_hi)):
        qm = jnp.where(keep, q_pair, jnp.zeros_like(q_pair))
        logits = _dot_nt(qm, k_pair) + bias
        mx = jnp.max(logits, axis=-1, keepdims=True)
        if extra is not None:
            mx = jnp.maximum(mx, extra)
        p = jnp.exp(logits - mx)
        den = jnp.sum(p, axis=-1, keepdims=True)
        if extra is not None:
            den = den + jnp.exp(extra - mx)
        outs.append(_dot(p.astype(jnp.bfloat16), v_pair) / den)
    return jnp.where(low, outs[0], outs[1])


def _na_kernel(q_ref, k_ref, v_ref, bias_ref, o_ref):
    m = pl.program_id(1)
    rows = k_ref.shape[0] // GRID_W
    start_row = jnp.clip(m * NA_Q_ROWS - NA_WIN_H // 2, 0, rows - NA_K_ROWS)
    start = pl.multiple_of(start_row * GRID_W, GRID_W * NA_Q_ROWS)
    kwin = k_ref[pl.ds(start, NA_K), :]
    vwin = v_ref[pl.ds(start, NA_K), :]
    q = q_ref[...]
    for g in range(NA_HEADS // 2):
        sl = slice(g * LANES, (g + 1) * LANES)
        out = _pair_heads(q[:, sl], kwin[:, sl], vwin[:, sl], bias_ref[2 * g], bias_ref[2 * g + 1])
        o_ref[:, sl] = out.astype(o_ref.dtype)


def _na_case(m, nsteps):
    return jnp.where(m == 0, 0, jnp.where(m == nsteps - 1, 2, 1))


def _na_attn(q, k, v, bias):
    b, s, _ = q.shape
    nsteps = s // NA_Q
    return pl.pallas_call(
        _na_kernel,
        name="na_attn",
        grid=(b, nsteps),
        in_specs=[
            pl.BlockSpec((None, NA_Q, NA_DIM), lambda bi, m: (bi, m, 0)),
            pl.BlockSpec((None, s, NA_DIM), lambda bi, m: (bi, 0, 0)),
            pl.BlockSpec((None, s, NA_DIM), lambda bi, m: (bi, 0, 0)),
            pl.BlockSpec((None, NA_HEADS, NA_Q, NA_K), lambda bi, m: (_na_case(m, nsteps), 0, 0, 0)),
        ],
        out_specs=pl.BlockSpec((None, NA_Q, NA_DIM), lambda bi, m: (bi, m, 0)),
        out_shape=jax.ShapeDtypeStruct((b, s, NA_DIM), jnp.bfloat16),
        compiler_params=_params(("parallel", "arbitrary")),
    )(q, k, v, bias)


def _na_bias_tables(rpb, s):
    rows = s // GRID_W
    nsteps = rows // NA_Q_ROWS

    c = np.arange(GRID_W)[:, None]
    kc = np.arange(GRID_W)[None, :]
    c0 = np.clip(c - NA_WIN_W // 2, 0, GRID_W - NA_WIN_W)
    valid_c = (kc >= c0) & (kc < c0 + NA_WIN_W)
    sel_c = (kc - c + NA_WIN_W - 1)[..., None] == np.arange(2 * NA_WIN_W - 1)

    def row_geometry(m):
        start_row = int(np.clip(m * NA_Q_ROWS - NA_WIN_H // 2, 0, rows - NA_K_ROWS))
        r = (m * NA_Q_ROWS + np.arange(NA_Q_ROWS))[:, None]
        kr = (start_row + np.arange(NA_K_ROWS))[None, :]
        r0 = np.clip(r - NA_WIN_H // 2, 0, rows - NA_WIN_H)
        valid_r = (kr >= r0) & (kr < r0 + NA_WIN_H)
        sel_r = (kr - r + NA_WIN_H - 1)[..., None] == np.arange(2 * NA_WIN_H - 1)
        return valid_r, sel_r

    interior = row_geometry(1)
    for m in range(1, nsteps - 1):
        assert all(np.array_equal(a, b_) for a, b_ in zip(row_geometry(m), interior))
    tabs = []
    for valid_r, sel_r in (row_geometry(0), interior, row_geometry(nsteps - 1)):
        t = jnp.einsum("rka,abh,cjb->hrckj", sel_r.astype(np.float32), rpb.astype(jnp.float32),
                       sel_c.astype(np.float32), precision=lax.Precision.HIGHEST)
        valid = valid_r[:, None, :, None] & valid_c[None, :, None, :]
        tabs.append(jnp.where(valid[None], t, NEG_INF).reshape(NA_HEADS, NA_Q, NA_K))
    return jnp.stack(tabs)


def _swa_kernel(sink_ref, q_ref, k_ref, v_ref, bias_ref, o_ref):
    n = pl.program_id(1)
    s = k_ref.shape[0]
    start = pl.multiple_of(jnp.clip(n * SWA_Q - SWA_WINDOW, 0, s - SWA_K), SWA_WINDOW)
    kwin = k_ref[pl.ds(start, SWA_K), :]
    vwin = v_ref[pl.ds(start, SWA_K), :]
    q = q_ref[...]
    for g in range(SWA_GROUP):
        sl = slice(g * LANES, (g + 1) * LANES)
        out = _pair_heads(q[:, sl], kwin, vwin, bias_ref[2 * g], bias_ref[2 * g + 1],
                          sink_ref[g], sink_ref[g + SWA_GROUP])
        o_ref[:, sl] = out.astype(o_ref.dtype)


def _swa_attn(sink, q, k, v, bias):
    b, s, _ = q.shape
    nsteps = s // SWA_Q
    return pl.pallas_call(
        _swa_kernel,
        name="swa_attn",
        grid=(b, nsteps),
        in_specs=[
            pl.BlockSpec(memory_space=pltpu.SMEM),
            pl.BlockSpec((None, SWA_Q, SWA_DIM), lambda bi, n: (bi, n, 0)),
            pl.BlockSpec((None, s, SWA_KV_DIM), lambda bi, n: (bi, 0, 0)),
            pl.BlockSpec((None, s, SWA_KV_DIM), lambda bi, n: (bi, 0, 0)),
            pl.BlockSpec((None, SWA_HEADS, SWA_Q, SWA_K), lambda bi, n: (_na_case(n, nsteps), 0, 0, 0)),
        ],
        out_specs=pl.BlockSpec((None, SWA_Q, SWA_DIM), lambda bi, n: (bi, n, 0)),
        out_shape=jax.ShapeDtypeStruct((b, s, SWA_DIM), jnp.bfloat16),
        compiler_params=_params(("parallel", "arbitrary")),
    )(sink, q, k, v, bias)


def _t5_bucket(rel):
    nb = T5_BUCKETS // 2
    max_exact = nb // 2
    ret = (rel > 0).astype(np.int32) * nb
    n = np.abs(rel)
    large = max_exact + (np.log(np.maximum(n, 1) / max_exact) / np.log(T5_MAX_DIST / max_exact)
                         * (nb - max_exact)).astype(np.int32)
    large = np.minimum(large, nb - 1)
    return (ret + np.where(n < max_exact, n, large)).astype(np.int32)


_SWA_SLOT_HEADS = [h for g in range(SWA_GROUP) for h in (g, g + SWA_GROUP)]


def _swa_regroup(w, axis):
    blocks = [lax.slice_in_dim(w, h * HEAD_DIM, (h + 1) * HEAD_DIM, axis=axis) for h in _SWA_SLOT_HEADS]
    return jnp.concatenate(blocks, axis=axis)


def _swa_bias_tables(t5_bias, s):
    offsets = np.arange(-SWA_WINDOW, SWA_WINDOW + 1)
    sel = _t5_bucket(offsets)[:, None] == np.arange(T5_BUCKETS)
    bias_off = jnp.dot(sel.astype(np.float32), t5_bias.astype(jnp.float32), precision=lax.Precision.HIGHEST)
    bias_off = jnp.stack([bias_off[:, h] for h in _SWA_SLOT_HEADS])
    nsteps = s // SWA_Q
    period = SWA_Q + SWA_K - 1
    tabs = []
    for n in (0, 1, nsteps - 1):
        start = int(np.clip(n * SWA_Q - SWA_WINDOW, 0, s - SWA_K))
        delta = n * SWA_Q - start
        lead = SWA_Q - 1 + delta - SWA_WINDOW
        g = jnp.concatenate([jnp.full((SWA_HEADS, lead), NEG_INF, jnp.float32), bias_off,
                             jnp.full((SWA_HEADS, period - lead - 2 * SWA_WINDOW - 1), NEG_INF, jnp.float32)], axis=1)
        hankel = jnp.tile(g, (1, SWA_Q + 1))[:, :SWA_Q * (period + 1)].reshape(SWA_HEADS, SWA_Q, period + 1)
        tabs.append(jnp.flip(hankel[:, :, :SWA_K], axis=1))
    return jnp.stack(tabs)


def _mem_kv_kernel(mem_ref, g_ref, wk_ref, wv_ref, k_ref, v_ref):
    mn = _rms(mem_ref[...], g_ref[...]).astype(jnp.bfloat16)
    k_ref[...] = _dot(mn, wk_ref[...]).astype(jnp.bfloat16)
    v_ref[...] = _dot(mn, wv_ref[...]).astype(jnp.bfloat16)


def _mem_kv(mem, g, wk, wv):
    b, m, _ = mem.shape
    full = lambda shape: pl.BlockSpec(shape, lambda bi: (0,) * len(shape))
    return pl.pallas_call(
        _mem_kv_kernel,
        name="mem_kv",
        grid=(b,),
        in_specs=[pl.BlockSpec((None, m, D_MODEL), lambda bi: (bi, 0, 0)), full((1, D_MODEL)),
                  full((D_MODEL, XA_DIM)), full((D_MODEL, XA_DIM))],
        out_specs=[pl.BlockSpec((None, m, XA_DIM), lambda bi: (bi, 0, 0))] * 2,
        out_shape=[jax.ShapeDtypeStruct((b, m, XA_DIM), jnp.bfloat16)] * 2,
        compiler_params=_params(("parallel",)),
    )(mem, g, wk, wv)


def _mix_out_kernel(x_ref, yna_ref, ysw_ref, u_ref, uprev_ref, unext_ref, b_ref, cw_ref,
                    wna_ref, wsc_ref, wsw_ref, g_ref, wq_ref, km_ref, vm_ref, wo_ref, o_ref):
    i = pl.program_id(1)
    nt = pl.num_programs(1)
    u = u_ref[...]
    t = u.shape[0]
    row = lax.broadcasted_iota(jnp.int32, (t, 1), 0)
    prev_row = jnp.where(i > 0, uprev_ref[SUBLANES - 1:SUBLANES, :], 0.0)
    next_row = jnp.where(i < nt - 1, unext_ref[0:1, :], 0.0)
    u_m1 = jnp.where(row == 0, prev_row, pltpu.roll(u, 1, axis=0))
    u_p1 = jnp.where(row == t - 1, next_row, pltpu.roll(u, t - 1, axis=0))
    ysc = b_ref[...] * (u_m1 * cw_ref[0:1, :] + u * cw_ref[1:2, :] + u_p1 * cw_ref[2:3, :])
    y = _dot(yna_ref[...], wna_ref[...]) + _dot(ysc.astype(jnp.bfloat16), wsc_ref[...]) \
        + _dot(ysw_ref[...], wsw_ref[...])
    x1 = x_ref[...] + y

    xn = _rms(x1, g_ref[...]).astype(jnp.bfloat16)
    q = _dot(xn, wq_ref[...])
    outs = []
    for h in range(XA_HEADS):
        sl = slice(h * XA_HEAD_DIM, (h + 1) * XA_HEAD_DIM)
        logits = _dot_nt(q[:, sl].astype(jnp.bfloat16), km_ref[:, sl]) * (XA_HEAD_DIM ** -0.5)
        mx = jnp.max(logits, axis=-1, keepdims=True)
        p = jnp.exp(logits - mx)
        den = jnp.sum(p, axis=-1, keepdims=True)
        outs.append((_dot(p.astype(jnp.bfloat16), vm_ref[:, sl]) / den).astype(jnp.bfloat16))
    o = jnp.concatenate(outs, axis=-1)
    o_ref[...] = x1 + _dot(o, wo_ref[...])


def _mix_out(x, yna, ysw, u, bgate, cw, wna, wsc, wsw, g, wq, km, vm, wo):
    b, s, _ = x.shape
    nt = s // TOK_TILE
    halo_blocks = TOK_TILE // SUBLANES
    last_halo = s // SUBLANES - 1
    tok = lambda d: pl.BlockSpec((None, TOK_TILE, d), lambda bi, i: (bi, i, 0))
    full = lambda shape: pl.BlockSpec(shape, lambda bi, i: (0,) * len(shape))
    mem = pl.BlockSpec((None, km.shape[1], XA_DIM), lambda bi, i: (bi, 0, 0))
    return pl.pallas_call(
        _mix_out_kernel,
        name="mix_out",
        grid=(b, nt),
        in_specs=[
            tok(D_MODEL), tok(NA_DIM), tok(SWA_DIM), tok(SC_DIM),
            pl.BlockSpec((None, SUBLANES, SC_DIM), lambda bi, i: (bi, jnp.maximum(i * halo_blocks - 1, 0), 0)),
            pl.BlockSpec((None, SUBLANES, SC_DIM), lambda bi, i: (bi, jnp.minimum((i + 1) * halo_blocks, last_halo), 0)),
            tok(SC_DIM), full((3, SC_DIM)),
            full((NA_DIM, D_MODEL)), full((SC_DIM, D_MODEL)), full((SWA_DIM, D_MODEL)),
            full((1, D_MODEL)), full((D_MODEL, XA_DIM)), mem, mem, full((XA_DIM, D_MODEL)),
        ],
        out_specs=tok(D_MODEL),
        out_shape=jax.ShapeDtypeStruct((b, s, D_MODEL), jnp.float32),
        compiler_params=_params(("parallel", "parallel")),
    )(x, yna, ysw, u, u, u, bgate, cw, wna, wsc, wsw, g, wq, km, vm, wo)


_CAND_PAIRS = [(p, q) for p in range(PEER_TOPK) for q in range(PEER_TOPK) if (p + 1) * (q + 1) <= PEER_TOPK]
_CAND_VREGS = -(-len(_CAND_PAIRS) // SUBLANES)


def _allmax_sublanes(v):
    for shift in (4, 2, 1):
        v = jnp.maximum(v, pltpu.roll(v, shift, axis=0))
    return v


def _top_values(s3, count):
    vals = []
    for r in range(count):
        m = _allmax_sublanes(jnp.max(s3, axis=0))
        vals.append(m)
        if r + 1 < count:
            s3 = jnp.where(s3 == m[None], -jnp.inf, s3)
    return vals


def _batcher_pairs(n):
    pairs = []
    p = 1
    while p < n:
        k = p
        while k >= 1:
            for j in range(k % p, n - k, 2 * k):
                for i in range(min(k, n - j - k)):
                    if (i + j) // (2 * p) == (i + j + k) // (2 * p):
                        pairs.append((i + j, i + j + k))
            k //= 2
        p *= 2
    return pairs


_TIE_SLACK = 1.0 - 2.0 ** -18
_SORT16 = _batcher_pairs(PEER_TOPK)
_BITONIC16 = [(i, i + k) for k in (8, 4, 2, 1) for i in range(PEER_TOPK) if not i & k]


def _descending(vals, pairs):
    vals = list(vals)
    for i, j in pairs:
        vals[i], vals[j] = jnp.maximum(vals[i], vals[j]), jnp.minimum(vals[i], vals[j])
    return vals


def _top16_sorted(s3):
    vals = _descending([s3[k] for k in range(PEER_TOPK)], _SORT16)
    for shift in (4, 2, 1):
        other = [pltpu.roll(v, shift, axis=0) for v in vals]
        vals = _descending([jnp.maximum(vals[k], other[PEER_TOPK - 1 - k]) for k in range(PEER_TOPK)], _BITONIC16)
    return vals


def _peer_prep_chunk(s0, s1):
    groups = PEER_NKEYS // SUBLANES
    a = _top16_sorted(s0.reshape(groups, SUBLANES, LANES))
    b = _top16_sorted(s1.reshape(groups, SUBLANES, LANES))
    sub = lax.broadcasted_iota(jnp.int32, (SUBLANES, LANES), 0)
    cand = []
    for v in range(_CAND_VREGS):
        acc = jnp.full((SUBLANES, LANES), -jnp.inf, jnp.float32)
        for k, (p, q) in enumerate(_CAND_PAIRS[v * SUBLANES:(v + 1) * SUBLANES]):
            acc = jnp.where(sub == k, a[p] + b[q], acc)
        cand.append(acc)
    best = _top_values(jnp.stack(cand), PEER_TOPK)
    z = jnp.ones((SUBLANES, LANES), jnp.float32)
    for r in range(1, PEER_TOPK):
        z = z + jnp.exp(best[r] - best[0])
    tau = best[PEER_TOPK - 1][0:1, :]
    floor = jnp.exp((tau - b[0][0:1, :]) - s0) * _TIE_SLACK
    rowgate = jnp.exp(s0 - a[0][0:1, :]) * (0.5 / z[0:1, :])
    colgate = jnp.exp(s1 - b[0][0:1, :])
    return floor, rowgate, colgate


def _row(words, k):
    return jnp.broadcast_to(words[k:k + 1, :], (PEER_NKEYS, LANES))


def _peer_gate_rows(eb, rows, floor_ref, rowg_ref, colg_ref, tile_ref, gt_ref):
    base = rows[0] - rows[0] % SUBLANES
    assert all(base <= ii < base + SUBLANES for ii in rows)
    i0 = pl.multiple_of(eb * (PEER_EB // PEER_NKEYS) + base, SUBLANES)
    for c in range(PEER_CHUNKS):
        w = [jnp.zeros((PEER_NKEYS, LANES), jnp.float32) for _ in rows]
        for h in range(PEER_HEADS):
            colgate = colg_ref[h, c]
            floors = floor_ref[h, c, pl.ds(i0, SUBLANES), :]
            rowgates = rowg_ref[h, c, pl.ds(i0, SUBLANES), :]
            for n, ii in enumerate(rows):
                w[n] = w[n] + jnp.where(colgate >= _row(floors, ii - base), colgate, 0.0) * _row(rowgates, ii - base)
        for n, ii in enumerate(rows):
            gt_ref[ii * PEER_NKEYS:(ii + 1) * PEER_NKEYS, c * LANES:(c + 1) * LANES] = (w[n] * tile_ref[ii, c]).astype(jnp.bfloat16)


def _peer_kernel(x_ref, g_ref, wqt_ref, sk_ref, u_ref, vt_ref, fg_ref, o_ref,
                 xnt_ref, qt_ref, tile_ref, floor_ref, rowg_ref, colg_ref, gt_ref, acc_ref, *, final_norm):
    eb = pl.program_id(2)
    neb = pl.num_programs(2)

    @pl.when(eb == 0)
    def _prep():
        xn = _rms(x_ref[...], g_ref[...])
        xnt_ref[...] = xn.T.astype(jnp.bfloat16)
        acc_ref[...] = jnp.zeros_like(acc_ref)

        q_rows = 2 * PEER_HEADS * PEER_HALF // PEER_Q_PIECES
        for k in range(PEER_Q_PIECES):
            rows = slice(k * q_rows, (k + 1) * q_rows)
            qt_ref[rows, :] = _dot(wqt_ref[rows, :], xnt_ref[...]).astype(jnp.bfloat16)

        for hp in range(2 * PEER_HEADS):
            s = _dot(sk_ref[hp], qt_ref[hp * PEER_HALF:(hp + 1) * PEER_HALF, :])
            for c in range(PEER_CHUNKS):
                tile_ref[hp, c] = s[:, c * LANES:(c + 1) * LANES]

        def head_gates(h, carry):
            for c in range(PEER_CHUNKS):
                floor_ref[h, c], rowg_ref[h, c], colg_ref[h, c] = _peer_prep_chunk(
                    tile_ref[2 * h, c], tile_ref[2 * h + 1, c])
            return carry

        lax.fori_loop(0, PEER_HEADS, head_gates, 0)

    for i0 in range(0, PEER_EB // PEER_NKEYS, PEER_H_ROWS):
        hidden = _dot(u_ref[i0 * PEER_NKEYS:(i0 + PEER_H_ROWS) * PEER_NKEYS, :], xnt_ref[...])
        act = hidden + hidden * lax.erf(hidden * (2.0 ** -0.5))
        for k in range(PEER_H_ROWS):
            for c in range(PEER_CHUNKS):
                tile_ref[i0 + k, c] = act[k * PEER_NKEYS:(k + 1) * PEER_NKEYS, c * LANES:(c + 1) * LANES]

    for r0 in range(0, PEER_EB // PEER_NKEYS, PEER_GATE_ROWS):
        _peer_gate_rows(eb, list(range(r0, r0 + PEER_GATE_ROWS)), floor_ref, rowg_ref, colg_ref, tile_ref, gt_ref)
    acc_ref[...] += _dot(vt_ref[...], gt_ref[...])

    @pl.when(eb == neb - 1)
    def _finish():
        y = x_ref[...] + acc_ref[...].T
        if final_norm:
            y = _rms(y, fg_ref[...])
        o_ref[...] = y


def _peer(x, g, wqt, sk, u, vt, fg, final_norm):
    b, s, _ = x.shape
    tok = pl.BlockSpec((None, PEER_TOK, D_MODEL), lambda bi, i, e: (bi, i, 0))
    full = lambda shape: pl.BlockSpec(shape, lambda bi, i, e: (0,) * len(shape), pipeline_mode=pl.Buffered(1))
    gate_scratch = pltpu.VMEM((PEER_HEADS, PEER_CHUNKS, PEER_NKEYS, LANES), jnp.float32)
    return pl.pallas_call(
        functools.partial(_peer_kernel, final_norm=final_norm),
        name="peer",
        grid=(b, s // PEER_TOK, PEER_EXPERTS // PEER_EB),
        in_specs=[
            tok, full((1, D_MODEL)), full((2 * PEER_HEADS * PEER_HALF, D_MODEL)),
            full((2 * PEER_HEADS, PEER_NKEYS, PEER_HALF)),
            pl.BlockSpec((PEER_EB, D_MODEL), lambda bi, i, e: (e, 0)),
            pl.BlockSpec((D_MODEL, PEER_EB), lambda bi, i, e: (0, e)),
            full((1, D_MODEL)),
        ],
        out_specs=tok,
        out_shape=jax.ShapeDtypeStruct((b, s, D_MODEL), jnp.float32),
        scratch_shapes=[
            pltpu.VMEM((D_MODEL, PEER_TOK), jnp.bfloat16),
            pltpu.VMEM((2 * PEER_HEADS * PEER_HALF, PEER_TOK), jnp.bfloat16),
            pltpu.VMEM((max(2 * PEER_HEADS, PEER_EB // PEER_NKEYS), PEER_CHUNKS, PEER_NKEYS, LANES), jnp.float32),
            gate_scratch, gate_scratch, gate_scratch,
            pltpu.VMEM((PEER_EB, PEER_TOK), jnp.bfloat16),
            pltpu.VMEM((D_MODEL, PEER_TOK), jnp.float32),
        ],
        compiler_params=_params(("parallel", "parallel", "arbitrary")),
    )(x, g, wqt, sk, u, vt, fg)


def _prepare_layer(l, w_in, conv_w, swa_sink, w_out, norm_mix_g, norm_xa_g, norm_mem_g, w_xq, w_xk, w_xv, w_xo,
                   norm_ffn_g, peer_wq, peer_subkeys, peer_u, peer_v):
    bf = jnp.bfloat16
    scale = HEAD_DIM ** -0.5
    o = np.cumsum([0, NA_DIM, NA_DIM, NA_DIM, SC_DIM, SC_DIM, SC_DIM, SWA_DIM, SWA_KV_DIM, SWA_KV_DIM])
    wi = w_in[l]
    w_in_l = jnp.concatenate([
        wi[:, o[0]:o[1]] * scale, wi[:, o[1]:o[6]], _swa_regroup(wi[:, o[6]:o[7]] * scale, 1), wi[:, o[7]:o[9]],
    ], axis=1).astype(bf)
    wo = w_out[l]
    row = lambda v: v.reshape(1, -1).astype(jnp.float32)
    return dict(
        g_mix=row(norm_mix_g[l]), w_in=w_in_l, conv_w=conv_w[l].astype(jnp.float32),
        sink=swa_sink[l].astype(jnp.float32),
        wna=wo[:NA_DIM].astype(bf), wsc=wo[NA_DIM:NA_DIM + SC_DIM].astype(bf),
        wsw=_swa_regroup(wo[NA_DIM + SC_DIM:], 0).astype(bf),
        g_xa=row(norm_xa_g[l]), g_mem=row(norm_mem_g[l]),
        wxq=w_xq[l].astype(bf), wxk=w_xk[l].astype(bf), wxv=w_xv[l].astype(bf), wxo=w_xo[l].astype(bf),
        g_ffn=row(norm_ffn_g[l]), wqt=peer_wq[l].T.astype(bf),
        sk=peer_subkeys[l].reshape(2 * PEER_HEADS, PEER_NKEYS, PEER_HALF).astype(bf),
        u=peer_u[l].astype(bf), vt=peer_v[l].T.astype(bf),
    )


def _trunk(x, mem, layers, na_bias, swa_bias, final_g):
    depth = len(layers)
    for l, p in enumerate(layers):
        naq, nak, nav, u, bgate, swq, swk, swv = _mix_in(x, p["g_mix"], p["w_in"])
        yna = _na_attn(naq, nak, nav, na_bias[l])
        ysw = _swa_attn(p["sink"], swq, swk, swv, swa_bias)
        km, vm = _mem_kv(mem, p["g_mem"], p["wxk"], p["wxv"])
        x = _mix_out(x, yna, ysw, u, bgate, p["conv_w"], p["wna"], p["wsc"], p["wsw"],
                     p["g_xa"], p["wxq"], km, vm, p["wxo"])
        x = _peer(x, p["g_ffn"], p["wqt"], p["sk"], p["u"], p["vt"], final_g, final_norm=(l == depth - 1))
    return x


def kernel(x_prompt, x_sample, mem_prompt, mem_sample, norm_mix_g, w_in, na_rpb, conv_w, swa_sink, t5_bias, w_out,
           norm_xa_g, norm_mem_g, w_xq, w_xk, w_xv, w_xo, norm_ffn_g, peer_wq, peer_subkeys, peer_u, peer_v, final_g):
    depth = w_in.shape[0]
    layers = [_prepare_layer(l, w_in, conv_w, swa_sink, w_out, norm_mix_g, norm_xa_g, norm_mem_g, w_xq, w_xk, w_xv,
                             w_xo, norm_ffn_g, peer_wq, peer_subkeys, peer_u, peer_v) for l in range(depth)]
    fg = final_g.reshape(1, -1).astype(jnp.float32)
    tables = {}
    outs = []
    for x, mem in ((x_prompt, mem_prompt), (x_sample, mem_sample)):
        s = x.shape[1]
        if s not in tables:
            tables[s] = ([_na_bias_tables(na_rpb[l], s) for l in range(depth)], _swa_bias_tables(t5_bias, s))
        outs.append(_trunk(x, mem, layers, *tables[s], fg))
    return tuple(outs)
```
